```python
import math
import jax
import jax.numpy as jnp
from jax import lax
import numpy as np

D_MODEL = 1024
BATCH = 16
SEQ = 2048
DEPTH = 2

N_A_LAYERS = DEPTH // 2
N_B_LAYERS = DEPTH - N_A_LAYERS

SSM_EXPAND = 2
D_INNER = SSM_EXPAND * D_MODEL
SSM_HEAD_DIM = 64
SSM_HEADS = D_INNER // SSM_HEAD_DIM
SSM_GROUPS = 4
SSM_HEADS_PER_GROUP = SSM_HEADS // SSM_GROUPS
SSM_STATE = 128
SSM_CONV = 4
SSM_CHUNK = 128
SSM_CONV_DIM = D_INNER + 2 * SSM_GROUPS * SSM_STATE
SSM_IN_DIM = D_INNER + SSM_CONV_DIM + SSM_HEADS

ATT_HEAD_DIM = 64
ATT_HEADS = D_MODEL // (2 * ATT_HEAD_DIM)
ATT_V_DIM = 2 * ATT_HEAD_DIM
ATT_BLOCK = 128
K_DIM = ATT_HEADS * 2 * ATT_HEAD_DIM
V_TOTAL = ATT_HEADS * ATT_V_DIM

NUM_BUCKETS = 32
MAX_DISTANCE = 128

FFN_DIM = 256 * ((8 * D_MODEL // 3 + 255) // 256)
FFN_CONV = 3

EPS = 1e-6

kernel_name = 'yoco_mamba2_diffattn_convffn'


def rms_norm(x, g):
    xf = x.astype(jnp.float32)
    y = xf * lax.rsqrt(jnp.mean(xf * xf, axis=-1, keepdims=True) + EPS)
    return y.astype(x.dtype) * g


def causal_dwconv(u, w, b):
    k_w = w.shape[0]
    s = u.shape[1]
    up = jnp.pad(u, ((0, 0), (k_w - 1, 0), (0, 0)))
    y = up[:, 0:s] * w[0]
    for k in range(1, k_w):
        y = y + up[:, k:k + s] * w[k]
    return y + b


def segsum(a):
    t = a.shape[-1]
    cs = jnp.cumsum(a, axis=-1)
    d = cs[..., :, None] - cs[..., None, :]
    mask = jnp.tril(jnp.ones((t, t), dtype=bool))
    return jnp.where(mask, d, -jnp.inf)


def ssd_chunked(xdt, a, bm, cm):
    b, s, h, p = xdt.shape
    g, r, l = SSM_GROUPS, SSM_HEADS_PER_GROUP, SSM_CHUNK
    c = s // l
    xc = xdt.reshape(b, c, l, g, r, p)
    ac = a.reshape(b, c, l, g, r).transpose(0, 3, 4, 1, 2)
    bc = bm.reshape(b, c, l, g, SSM_STATE)
    cc = cm.reshape(b, c, l, g, SSM_STATE)
    a_cs = jnp.cumsum(ac, axis=-1)
    decay_in = jnp.exp(segsum(ac))
    cb = jnp.einsum('bclgn,bcsgn->bcgls', cc, bc)
    y_diag = jnp.einsum('bcgls,bgrcls,bcsgrp->bclgrp', cb, decay_in, xc)
    decay_states = jnp.exp(a_cs[..., -1:] - a_cs)
    states = jnp.einsum('bclgn,bgrcl,bclgrp->bcgrpn', bc, decay_states, xc)
    states0 = jnp.concatenate([jnp.zeros_like(states[:, :1]), states], axis=1)
    chunk_tot = jnp.pad(a_cs[..., -1], ((0, 0), (0, 0), (0, 0), (1, 0)))
    decay_chunk = jnp.exp(segsum(chunk_tot))
    new_states = jnp.einsum('bgrzc,bcgrpn->bzgrpn', decay_chunk, states0)
    states_in = new_states[:, :-1]
    y_off = jnp.einsum('bclgn,bcgrpn,bgrcl->bclgrp', cc, states_in, jnp.exp(a_cs))
    return (y_diag + y_off).reshape(b, s, h, p)


def mamba2_mixer(x, ln_g, in_w, conv_w, conv_b, dt_bias, a_log, d_skip, norm_g, out_w):
    b, s, _ = x.shape
    zxbcdt = rms_norm(x, ln_g) @ in_w
    z = zxbcdt[..., :D_INNER]
    xbc = zxbcdt[..., D_INNER:D_INNER + SSM_CONV_DIM]
    dt = zxbcdt[..., D_INNER + SSM_CONV_DIM:]
    xbc = jax.nn.silu(causal_dwconv(xbc, conv_w, conv_b))
    gn = SSM_GROUPS * SSM_STATE
    xs = xbc[..., :D_INNER].reshape(b, s, SSM_HEADS, SSM_HEAD_DIM)
    bm = xbc[..., D_INNER:D_INNER + gn].reshape(b, s, SSM_GROUPS, SSM_STATE)
    cm = xbc[..., D_INNER + gn:].reshape(b, s, SSM_GROUPS, SSM_STATE)
    dt = jax.nn.softplus(dt.astype(jnp.float32) + dt_bias.astype(jnp.float32))
    a = dt * (-jnp.exp(a_log.astype(jnp.float32)))
    xs32 = xs.astype(jnp.float32)
    y = ssd_chunked(xs32 * dt[..., None], a, bm.astype(jnp.float32), cm.astype(jnp.float32))
    y = y + xs32 * d_skip.astype(jnp.float32)[:, None]
    y = y.reshape(b, s, D_INNER) * jax.nn.silu(z.astype(jnp.float32))
    y = rms_norm(y.reshape(b, s, SSM_GROUPS, D_INNER // SSM_GROUPS),
                 norm_g.reshape(SSM_GROUPS, D_INNER // SSM_GROUPS)).reshape(b, s, D_INNER)
    return y @ out_w


def conv_ffn(x, ln_g, up_w, conv_w, conv_b, down_w):
    u = causal_dwconv(rms_norm(x, ln_g) @ up_w, conv_w, conv_b)
    gate, up = u[..., :FFN_DIM], u[..., FFN_DIM:]
    return (jax.nn.silu(gate) * up) @ down_w


def shared_kv(x, kv_ln_g, kv_w, k_norm_g):
    b, s, _ = x.shape
    kv = rms_norm(x, kv_ln_g) @ kv_w
    k = kv[..., :K_DIM].reshape(b, s, ATT_HEADS, 2, ATT_HEAD_DIM)
    v = kv[..., K_DIM:].reshape(b, s, ATT_HEADS, ATT_V_DIM)
    return rms_norm(k, k_norm_g), v


def t5_bucket(dist):
    n = jnp.maximum(dist, 0)
    max_exact = NUM_BUCKETS // 2
    nf = jnp.maximum(n, 1).astype(jnp.float32)
    large = max_exact + (jnp.log(nf / max_exact) / math.log(MAX_DISTANCE / max_exact)
                         * (NUM_BUCKETS - max_exact)).astype(jnp.int32)
    large = jnp.minimum(large, NUM_BUCKETS - 1)
    return jnp.where(n < max_exact, n, large)


def diff_attention(x, k, v, rel_bias, ln_g, q_w, q_norm_g, lam_vecs, subln_g, out_w, lam_init):
    b, s, _ = x.shape
    q = (rms_norm(x, ln_g) @ q_w).reshape(b, s, ATT_HEADS, 2, ATT_HEAD_DIM)
    q = rms_norm(q, q_norm_g)
    lv = lam_vecs.astype(jnp.float32)
    lam = jnp.exp(jnp.sum(lv[0] * lv[1])) - jnp.exp(jnp.sum(lv[2] * lv[3])) + lam_init
    scale = ATT_HEAD_DIM ** -0.5
    nb = s // ATT_BLOCK
    qb = q.reshape(b, nb, ATT_BLOCK, ATT_HEADS, 2, ATT_HEAD_DIM).transpose(1, 0, 2, 3, 4, 5)
    kpos = jnp.arange(s)

    def block(args):
        qblk, i = args
        qpos = i * ATT_BLOCK + jnp.arange(ATT_BLOCK)
        dist = qpos[:, None] - kpos[None, :]
        bias = rel_bias[t5_bucket(dist)].astype(jnp.float32).transpose(2, 0, 1)
        logits = jnp.einsum('bqhcd,bkhcd->bhcqk', qblk, k).astype(jnp.float32) * scale
        logits = logits + bias[None, :, None]
        logits = jnp.where(dist >= 0, logits, -jnp.inf)
        p = jax.nn.softmax(logits, axis=-1)
        attn = p[:, :, 0] - lam * p[:, :, 1]
        return jnp.einsum('bhqk,bkhd->bqhd', attn.astype(v.dtype), v)

    o = lax.map(block, (qb, jnp.arange(nb)))
    o = o.transpose(1, 0, 2, 3, 4).reshape(b, s, ATT_HEADS, ATT_V_DIM)
    o = rms_norm(o, subln_g) * (1.0 - lam_init)
    return o.reshape(b, s, V_TOTAL) @ out_w


def setup_inputs(seed: int = 0) -> dict:
    key = jax.random.key(seed)
    ks = jax.random.split(key, 32)
    f32 = jnp.float32

    def nrm(k, shape, scale):
        return jax.random.normal(k, shape, f32) * scale

    def gain(k, shape):
        return 1.0 + 0.02 * jax.random.normal(k, shape, f32)

    na, nb = N_A_LAYERS, N_B_LAYERS
    dt = jnp.exp(jax.random.uniform(ks[6], (na, SSM_HEADS), f32)
                 * (math.log(0.1) - math.log(0.001)) + math.log(0.001))
    return {
        'x': nrm(ks[0], (BATCH, SEQ, D_MODEL), 1.0),
        'ssm_ln_g': gain(ks[1], (na, D_MODEL)),
        'ssm_in_w': nrm(ks[2], (na, D_MODEL, SSM_IN_DIM), D_MODEL ** -0.5),
        'ssm_conv_w': nrm(ks[3], (na, SSM_CONV, SSM_CONV_DIM), SSM_CONV ** -0.5),
        'ssm_conv_b': nrm(ks[4], (na, SSM_CONV_DIM), 0.02),
        'ssm_dt_bias': dt + jnp.log(-jnp.expm1(-dt)),
        'ssm_a_log': jnp.log(jax.random.uniform(ks[7], (na, SSM_HEADS), f32, 1.0, 16.0)),
        'ssm_d': gain(ks[8], (na, SSM_HEADS)),
        'ssm_norm_g': gain(ks[9], (na, D_INNER)),
        'ssm_out_w': nrm(ks[10], (na, D_INNER, D_MODEL), D_INNER ** -0.5),
        'kv_ln_g': gain(ks[11], (D_MODEL,)),
        'kv_w': nrm(ks[12], (D_MODEL, K_DIM + V_TOTAL), D_MODEL ** -0.5),
        'k_norm_g': gain(ks[13], (ATT_HEAD_DIM,)),
        'rel_bias': nrm(ks[14], (NUM_BUCKETS, ATT_HEADS), 0.5),
        'attn_ln_g': gain(ks[15], (nb, D_MODEL)),
        'q_w': nrm(ks[16], (nb, D_MODEL, K_DIM), D_MODEL ** -0.5),
        'q_norm_g': gain(ks[17], (nb, ATT_HEAD_DIM)),
        'lam_vecs': nrm(ks[18], (nb, 4, ATT_HEAD_DIM), 0.1),
        'subln_g': gain(ks[19], (nb, ATT_V_DIM)),
        'attn_out_w': nrm(ks[20], (nb, V_TOTAL, D_MODEL), V_TOTAL ** -0.5),
        'ffn_ln_g': gain(ks[21], (DEPTH, D_MODEL)),
        'ffn_up_w': nrm(ks[22], (DEPTH, D_MODEL, 2 * FFN_DIM), D_MODEL ** -0.5),
        'ffn_conv_w': nrm(ks[23], (DEPTH, FFN_CONV, 2 * FFN_DIM), FFN_CONV ** -0.5),
        'ffn_conv_b': nrm(ks[24], (DEPTH, 2 * FFN_DIM), 0.02),
        'ffn_down_w': nrm(ks[25], (DEPTH, FFN_DIM, D_MODEL), FFN_DIM ** -0.5),
    }


def reference(x, ssm_ln_g, ssm_in_w, ssm_conv_w, ssm_conv_b, ssm_dt_bias, ssm_a_log, ssm_d,
              ssm_norm_g, ssm_out_w, kv_ln_g, kv_w, k_norm_g, rel_bias, attn_ln_g, q_w,
              q_norm_g, lam_vecs, subln_g, attn_out_w, ffn_ln_g, ffn_up_w, ffn_conv_w,
              ffn_conv_b, ffn_down_w):
    k_sh, v_sh = None, None
    for layer in range(DEPTH):
        if layer < N_A_LAYERS:
            i = layer
            mix = mamba2_mixer(x, ssm_ln_g[i], ssm_in_w[i], ssm_conv_w[i], ssm_conv_b[i],
                               ssm_dt_bias[i], ssm_a_log[i], ssm_d[i], ssm_norm_g[i],
                               ssm_out_w[i])
        else:
            j = layer - N_A_LAYERS
            if j == 0:
                k_sh, v_sh = shared_kv(x, kv_ln_g, kv_w, k_norm_g)
            lam_init = 0.8 - 0.6 * math.exp(-0.3 * layer)
            mix = diff_attention(x, k_sh, v_sh, rel_bias, attn_ln_g[j], q_w[j], q_norm_g[j],
                                 lam_vecs[j], subln_g[j], attn_out_w[j], lam_init)
        x = x + mix.astype(x.dtype)
        ff = conv_ffn(x, ffn_ln_g[layer], ffn_up_w[layer], ffn_conv_w[layer],
                      ffn_conv_b[layer], ffn_down_w[layer])
        x = x + ff.astype(x.dtype)
    return x
```

```python
import functools
import math

import numpy as np
import jax
import jax.numpy as jnp
from jax import lax
from jax.experimental import pallas as pl
from jax.experimental.pallas import tpu as pltpu

F32 = jnp.float32
BF16 = jnp.bfloat16

EPS = 1e-6
NEG = -1e30

SSM_GROUPS = 4
SSM_STATE = 128
SSM_HEAD_DIM = 64
SSM_CHUNK = 128
ATT_HEAD_DIM = 64
NUM_BUCKETS = 32
MAX_DISTANCE = 128
LANES = 128
SUBLANES = 8
VMEM_LIMIT = 56 * 1024 * 1024

ROW_TILE = 256
ATT_TILE = 256


def _params(*sem):
    return pltpu.CompilerParams(dimension_semantics=sem, vmem_limit_bytes=VMEM_LIMIT)


def _const_spec(shape):
    nd = len(shape)
    return pl.BlockSpec(shape, lambda *_: (0,) * nd, pipeline_mode=pl.Buffered(1))


def _rms(x, g):
    ms = jnp.mean(x * x, axis=-1, keepdims=True)
    return x * lax.rsqrt(ms + EPS) * g


def _silu(x):
    return x * (1.0 / (1.0 + jnp.exp(-x)))


def _split3(v):
    hi = v.astype(BF16)
    r1 = v - hi.astype(F32)
    mid = r1.astype(BF16)
    lo = (r1 - mid.astype(F32)).astype(BF16)
    return hi, mid, lo


def _dot(a, b):
    return jnp.dot(a, b, preferred_element_type=F32)


def _dot_nt(a, b):
    return lax.dot_general(a, b, (((1,), (1,)), ((), ())), preferred_element_type=F32)


def _norm_proj_kernel(n_out, x_ref, g_ref, *refs):
    w_refs, o_refs = refs[:n_out], refs[n_out:]
    xn = _rms(x_ref[...], g_ref[...]).astype(BF16)
    for w_ref, o_ref in zip(w_refs, o_refs):
        o_ref[...] = _dot(xn, w_ref[...]).astype(o_ref.dtype)


def _norm_proj(x, g, ws, out_dtypes, name):
    t, d = x.shape
    tm = ROW_TILE
    in_specs = [pl.BlockSpec((tm, d), lambda i: (i, 0)), _const_spec((1, d))]
    in_specs += [_const_spec(w.shape) for w in ws]
    out_specs = [pl.BlockSpec((tm, w.shape[1]), lambda i: (i, 0)) for w in ws]
    out_shape = [jax.ShapeDtypeStruct((t, w.shape[1]), dt) for w, dt in zip(ws, out_dtypes)]
    return pl.pallas_call(
        functools.partial(_norm_proj_kernel, len(ws)),
        grid=(t // tm,), in_specs=in_specs, out_specs=out_specs, out_shape=out_shape,
        compiler_params=_params("arbitrary"), name=name,
    )(x, g.reshape(1, d), *ws)


def _proj_residual_kernel(y_ref, w_ref, r_ref, o_ref):
    o_ref[...] = r_ref[...] + _dot(y_ref[...], w_ref[...])


def _proj_residual(y, w, res, name):
    t, k = y.shape
    d = w.shape[1]
    tm = ROW_TILE
    return pl.pallas_call(
        _proj_residual_kernel, grid=(t // tm,),
        in_specs=[pl.BlockSpec((tm, k), lambda i: (i, 0)), _const_spec(w.shape),
                  pl.BlockSpec((tm, d), lambda i: (i, 0))],
        out_specs=pl.BlockSpec((tm, d), lambda i: (i, 0)),
        out_shape=jax.ShapeDtypeStruct((t, d), F32),
        compiler_params=_params("arbitrary"), name=name,
    )(y, w, res)


def _ssd_kernel(z_ref, xbc_ref, dt_ref, cw_ref, cb_ref, dtb_ref, alog_ref, dskip_ref, ng_ref,
                o_ref, cbuf, state, ybuf):
    l = SSM_CHUNK
    d_inner = z_ref.shape[1]
    n_pairs = d_inner // LANES
    pairs_per_group = n_pairs // SSM_GROUPS
    gn = SSM_GROUPS * SSM_STATE
    c = pl.program_id(1)
    halo = SUBLANES

    @pl.when(c == 0)
    def _():
        cbuf[0:halo, :] = jnp.zeros((halo, cbuf.shape[1]), F32)
        state[...] = jnp.zeros(state.shape, F32)

    @pl.when(c > 0)
    def _():
        cbuf[0:halo, :] = cbuf[l:l + halo, :]

    u = xbc_ref[...]
    cbuf[halo:halo + l, :] = u
    k_w = cw_ref.shape[0]
    conv = cw_ref[k_w - 1:k_w, :] * u + cb_ref[...]
    for s in range(1, k_w):
        conv = conv + cw_ref[k_w - 1 - s:k_w - s, :] * cbuf[halo - s:halo - s + l, :]
    act = _silu(conv)
    xs = act[:, :d_inner]
    bm = act[:, d_inner:d_inner + gn].astype(BF16)
    cm_f = act[:, d_inner + gn:]

    pre = dt_ref[...] + dtb_ref[...]
    dtv = jnp.maximum(pre, 0.0) + jnp.log(1.0 + jnp.exp(-jnp.abs(pre)))
    a = dtv * (-jnp.exp(alog_ref[...]))

    row = lax.broadcasted_iota(jnp.int32, (l, l), 0)
    col = lax.broadcasted_iota(jnp.int32, (l, l), 1)
    causal = row >= col
    tri = jnp.where(causal, 1.0, 0.0).astype(BF16)
    a_hi, a_mid, a_lo = _split3(a)
    cs = _dot(tri, a_hi) + _dot(tri, a_mid) + _dot(tri, a_lo)

    cs_t = cs.T
    dt_t = dtv.T
    tot = cs_t[:, l - 1:l]
    w_t = dt_t * jnp.exp(tot - cs_t)
    g_t = jnp.broadcast_to(jnp.exp(tot), (LANES, l))
    xs_t = xs.T
    lane = lax.broadcasted_iota(jnp.int32, (l, LANES), 1)
    sub = lax.broadcasted_iota(jnp.int32, (LANES, l), 0)
    lane_lo = lane < SSM_HEAD_DIM
    sub_lo = sub < SSM_HEAD_DIM

    for g in range(SSM_GROUPS):
        b_g = bm[:, g * SSM_STATE:(g + 1) * SSM_STATE]
        c_gf = cm_f[:, g * SSM_STATE:(g + 1) * SSM_STATE]
        cb = _dot_nt(c_gf.astype(BF16), b_g)
        for q in range(pairs_per_group):
            pi = g * pairs_per_group + q
            xs_pair = xs[:, pi * LANES:(pi + 1) * LANES]
            s_pair = state[pi]
            s_pair_b = s_pair.astype(BF16)
            y_pair = jnp.zeros((l, LANES), F32)
            for hh in range(2):
                h = 2 * pi + hh
                colb = jnp.broadcast_to(cs[:, h:h + 1], (l, l))
                rowb = jnp.broadcast_to(cs_t[h:h + 1, :], (l, l))
                dtrow = jnp.broadcast_to(dt_t[h:h + 1, :], (l, l))
                dec = jnp.exp(jnp.where(causal, colb - rowb, NEG))
                part1 = (dec * (cb * dtrow)).astype(BF16)
                part2 = (jnp.exp(colb) * c_gf).astype(BF16)
                mask = lane_lo if hh == 0 else jnp.logical_not(lane_lo)
                xs_m = jnp.where(mask, xs_pair, 0.0).astype(BF16)
                y_pair = y_pair + _dot(part1, xs_m)
                y_pair = y_pair + jnp.where(mask, _dot_nt(part2, s_pair_b), 0.0)
            ybuf[:, pi * LANES:(pi + 1) * LANES] = y_pair
            h0 = 2 * pi
            w2 = jnp.where(sub_lo, jnp.broadcast_to(w_t[h0:h0 + 1, :], (LANES, l)),
                           jnp.broadcast_to(w_t[h0 + 1:h0 + 2, :], (LANES, l)))
            g2 = jnp.where(sub_lo, jnp.broadcast_to(g_t[h0:h0 + 1, :], (LANES, l)),
                           jnp.broadcast_to(g_t[h0 + 1:h0 + 2, :], (LANES, l)))
            xst = xs_t[pi * LANES:(pi + 1) * LANES, :]
            state[pi] = g2 * s_pair + _dot((xst * w2).astype(BF16), b_g)

    y = ybuf[...] + xs * dskip_ref[...]
    y = y * _silu(z_ref[...])
    gw = d_inner // SSM_GROUPS
    for g in range(SSM_GROUPS):
        sl = slice(g * gw, (g + 1) * gw)
        o_ref[:, sl] = _rms(y[:, sl], ng_ref[:, sl]).astype(o_ref.dtype)


def _ssd(z, xbc, dt, cw, cb, dtb, alog, dskip, ng, batch, seq):
    t, d_inner = z.shape
    conv_dim = xbc.shape[1]
    l = SSM_CHUNK
    nc = seq // l
    row = lambda b, c: (b * nc + c, 0)
    return pl.pallas_call(
        _ssd_kernel, grid=(batch, nc),
        in_specs=[pl.BlockSpec((l, d_inner), row), pl.BlockSpec((l, conv_dim), row),
                  pl.BlockSpec((l, LANES), row),
                  _const_spec(cw.shape), _const_spec(cb.shape), _const_spec(dtb.shape),
                  _const_spec(alog.shape), _const_spec(dskip.shape), _const_spec(ng.shape)],
        out_specs=pl.BlockSpec((l, d_inner), row),
        out_shape=jax.ShapeDtypeStruct((t, d_inner), BF16),
        scratch_shapes=[pltpu.VMEM((l + SUBLANES, conv_dim), F32),
                        pltpu.VMEM((d_inner // LANES, LANES, SSM_STATE), F32),
                        pltpu.VMEM((l, d_inner), F32)],
        compiler_params=_params("arbitrary", "arbitrary"), name="ssd",
    )(z, xbc, dt, cw, cb, dtb, alog, dskip, ng)


def _ffn_kernel(tiles_per_seq, x_ref, g_ref, wup_ref, cw_ref, cb_ref, wdn_ref, o_ref, cbuf):
    tm = x_ref.shape[0]
    f = wdn_ref.shape[0]
    halo = SUBLANES
    i = pl.program_id(0)
    first = (i % tiles_per_seq) == 0

    @pl.when(first)
    def _():
        cbuf[0:halo, :] = jnp.zeros((halo, cbuf.shape[1]), F32)

    @pl.when(jnp.logical_not(first))
    def _():
        cbuf[0:halo, :] = cbuf[tm:tm + halo, :]

    x = x_ref[...]
    xn = _rms(x, g_ref[...]).astype(BF16)
    u = _dot(xn, wup_ref[...])
    cbuf[halo:halo + tm, :] = u
    k_w = cw_ref.shape[0]
    conv = cw_ref[k_w - 1:k_w, :] * u + cb_ref[...]
    for s in range(1, k_w):
        conv = conv + cw_ref[k_w - 1 - s:k_w - s, :] * cbuf[halo - s:halo - s + tm, :]
    hid = (_silu(conv[:, :f]) * conv[:, f:]).astype(BF16)
    o_ref[...] = x + _dot(hid, wdn_ref[...])


def _ffn(x, g, wup, cw, cb, wdn, seq, name):
    t, d = x.shape
    tm = ROW_TILE
    return pl.pallas_call(
        functools.partial(_ffn_kernel, seq // tm), grid=(t // tm,),
        in_specs=[pl.BlockSpec((tm, d), lambda i: (i, 0)), _const_spec((1, d)),
                  _const_spec(wup.shape), _const_spec(cw.shape), _const_spec((1, cb.shape[0])),
                  _const_spec(wdn.shape)],
        out_specs=pl.BlockSpec((tm, d), lambda i: (i, 0)),
        out_shape=jax.ShapeDtypeStruct((t, d), F32),
        scratch_shapes=[pltpu.VMEM((tm + SUBLANES, wup.shape[1]), F32)],
        compiler_params=_params("arbitrary"), name=name,
    )(x, g.reshape(1, d), wup, cw, cb.reshape(1, -1), wdn)


def _pair_norm(blk, g2):
    lane = lax.broadcasted_iota(jnp.int32, blk.shape, 1)
    lo = lane < ATT_HEAD_DIM
    sq = blk * blk
    s_lo = jnp.sum(jnp.where(lo, sq, 0.0), axis=-1, keepdims=True)
    s_hi = jnp.sum(jnp.where(lo, 0.0, sq), axis=-1, keepdims=True)
    inv = 1.0 / ATT_HEAD_DIM
    rs = jnp.where(lo, lax.rsqrt(s_lo * inv + EPS), lax.rsqrt(s_hi * inv + EPS))
    return blk * rs * g2


def _kv_kernel(x_ref, g_ref, w_ref, kg_ref, k_ref, v_ref):
    n_heads = k_ref.shape[0]
    xn = _rms(x_ref[...], g_ref[...]).astype(BF16)
    kv = _dot(xn, w_ref[...])
    kd = n_heads * LANES
    for h in range(n_heads):
        sl = slice(h * LANES, (h + 1) * LANES)
        k_ref[h] = _pair_norm(kv[:, sl], kg_ref[...]).astype(k_ref.dtype)
        v_ref[h] = kv[:, kd + h * LANES:kd + (h + 1) * LANES].astype(v_ref.dtype)


def _kv_proj(x, g, w, kg2, n_heads):
    t, d = x.shape
    tm = ROW_TILE
    hm = jax.ShapeDtypeStruct((n_heads, t, LANES), BF16)
    hspec = pl.BlockSpec((n_heads, tm, LANES), lambda i: (0, i, 0))
    return pl.pallas_call(
        _kv_kernel, grid=(t // tm,),
        in_specs=[pl.BlockSpec((tm, d), lambda i: (i, 0)), _const_spec((1, d)),
                  _const_spec(w.shape), _const_spec((1, LANES))],
        out_specs=[hspec, hspec], out_shape=[hm, hm],
        compiler_params=_params("arbitrary"), name="kv_proj",
    )(x, g.reshape(1, d), w, kg2)


def _bucket_thresholds():
    n = np.arange(1, 4 * MAX_DISTANCE, dtype=np.int64)
    max_exact = NUM_BUCKETS // 2
    nf = n.astype(np.float32)
    large = max_exact + (np.log(nf / np.float32(max_exact)) / np.float32(math.log(MAX_DISTANCE / max_exact))
                         * np.float32(NUM_BUCKETS - max_exact)).astype(np.int32)
    large = np.minimum(large, NUM_BUCKETS - 1)
    bucket = np.where(n < max_exact, n, large)
    return [int(n[np.argmax(bucket >= b)]) for b in range(max_exact + 1, NUM_BUCKETS)]


def _bias_kernel(thresholds, rb_ref, o_ref):
    t = o_ref.shape[2]
    h = pl.program_id(0)
    max_exact = NUM_BUCKETS // 2
    row = lax.broadcasted_iota(jnp.int32, (t, t), 0)
    col = lax.broadcasted_iota(jnp.int32, (t, t), 1)
    far = rb_ref[NUM_BUCKETS - 1, h]
    for kind in range(2):
        d = row - col + kind * t
        n = jnp.maximum(d, 0)
        bucket = jnp.minimum(n, max_exact)
        for thr in thresholds:
            bucket = bucket + jnp.where(n >= thr, 1, 0)
        val = jnp.zeros((t, t), F32)
        for b in range(NUM_BUCKETS):
            val = jnp.where(bucket == b, rb_ref[b, h] - far, val)
        if kind == 0:
            val = jnp.where(d >= 0, val, NEG)
        o_ref[0, kind] = val


def _bias_tiles(rel_bias, n_heads, t):
    return pl.pallas_call(
        functools.partial(_bias_kernel, _bucket_thresholds()), grid=(n_heads,),
        in_specs=[pl.BlockSpec(memory_space=pltpu.SMEM)],
        out_specs=pl.BlockSpec((1, 2, t, t), lambda h: (h, 0, 0, 0)),
        out_shape=jax.ShapeDtypeStruct((n_heads, 2, t, t), F32),
        compiler_params=_params("arbitrary"), name="rel_bias_tiles",
    )(rel_bias)


def _attn_kernel(lam_init, x_ref, g_ref, qw_ref, qg_ref, lv_ref, sg_ref, k_ref, v_ref, bias_ref,
                 o_ref, q_scr, m_scr, l_scr, acc_scr):
    t = x_ref.shape[0]
    n_heads = k_ref.shape[0]
    qi = pl.program_id(1)

    xn = _rms(x_ref[...], g_ref[...]).astype(BF16)
    q = _dot(xn, qw_ref[...])
    lane = lax.broadcasted_iota(jnp.int32, (t, LANES), 1)
    lo = lane < ATT_HEAD_DIM
    scale = ATT_HEAD_DIM ** -0.5
    for h in range(n_heads):
        qn = _pair_norm(q[:, h * LANES:(h + 1) * LANES], qg_ref[...]) * scale
        q_scr[h, 0:t, :] = jnp.where(lo, qn, 0.0).astype(BF16)
        q_scr[h, t:2 * t, :] = jnp.where(lo, 0.0, qn).astype(BF16)

    lv = lv_ref[...]
    lam = (jnp.exp(jnp.sum(lv[0:1] * lv[1:2], axis=-1, keepdims=True))
           - jnp.exp(jnp.sum(lv[2:3] * lv[3:4], axis=-1, keepdims=True)) + lam_init)

    def head_body(h, carry):
        m_scr[...] = jnp.full(m_scr.shape, NEG, F32)
        l_scr[...] = jnp.zeros(l_scr.shape, F32)
        acc_scr[...] = jnp.zeros(acc_scr.shape, F32)
        qh = q_scr[h]

        def step(j, bias):
            start = pl.multiple_of(j * t, t)
            kt = k_ref[h, pl.ds(start, t), :]
            vt = v_ref[h, pl.ds(start, t), :]
            s = _dot_nt(qh, kt)
            if bias is not None:
                s = jnp.concatenate([s[:t] + bias, s[t:] + bias], axis=0)
            m_prev = m_scr[...]
            m_new = jnp.maximum(m_prev, jnp.max(s, axis=-1, keepdims=True))
            alpha = jnp.exp(m_prev - m_new)
            p = jnp.exp(s - m_new)
            l_scr[...] = alpha * l_scr[...] + jnp.sum(p, axis=-1, keepdims=True)
            acc_scr[...] = alpha * acc_scr[...] + _dot(p.astype(BF16), vt)
            m_scr[...] = m_new

        def far_step(j, c):
            step(j, None)
            return c

        lax.fori_loop(0, jnp.maximum(qi - 1, 0), far_step, 0)

        @pl.when(qi >= 1)
        def _():
            step(qi - 1, bias_ref[h, 1])

        step(qi, bias_ref[h, 0])

        o = acc_scr[...] / l_scr[...]
        d = o[:t] - lam * o[t:]
        d = _rms(d, sg_ref[...]) * (1.0 - lam_init)
        o_ref[:, pl.ds(pl.multiple_of(h * LANES, LANES), LANES)] = d.astype(o_ref.dtype)
        return carry

    lax.fori_loop(0, n_heads, head_body, 0)


def _attention(x, g, qw, qg2, lam_vecs, sg, k, v, bias, lam_init, batch, seq):
    tt, d = x.shape
    n_heads = k.shape[0]
    t = ATT_TILE
    nq = seq // t
    kd = qw.shape[1]
    kv_spec = pl.BlockSpec((n_heads, seq, LANES), lambda b, i: (0, b, 0))
    return pl.pallas_call(
        functools.partial(_attn_kernel, lam_init), grid=(batch, nq),
        in_specs=[pl.BlockSpec((t, d), lambda b, i: (b * nq + i, 0)), _const_spec((1, d)),
                  _const_spec(qw.shape), _const_spec((1, LANES)), _const_spec(lam_vecs.shape),
                  _const_spec((1, LANES)), kv_spec, kv_spec, _const_spec(bias.shape)],
        out_specs=pl.BlockSpec((t, kd), lambda b, i: (b * nq + i, 0)),
        out_shape=jax.ShapeDtypeStruct((tt, kd), BF16),
        scratch_shapes=[pltpu.VMEM((n_heads, 2 * t, LANES), BF16),
                        pltpu.VMEM((2 * t, 1), F32), pltpu.VMEM((2 * t, 1), F32),
                        pltpu.VMEM((2 * t, LANES), F32)],
        compiler_params=_params("arbitrary", "arbitrary"), name="diff_attention",
    )(x, g.reshape(1, d), qw, qg2, lam_vecs, sg.reshape(1, LANES), k, v, bias)


def kernel(x, ssm_ln_g, ssm_in_w, ssm_conv_w, ssm_conv_b, ssm_dt_bias, ssm_a_log, ssm_d, ssm_norm_g, ssm_out_w, kv_ln_g, kv_w, k_norm_g, rel_bias, attn_ln_g, q_w, q_norm_g, lam_vecs, subln_g, attn_out_w, ffn_ln_g, ffn_up_w, ffn_conv_w, ffn_conv_b, ffn_down_w):
    batch, seq, d = x.shape
    t = batch * seq
    n_a = ssm_in_w.shape[0]
    depth = ffn_up_w.shape[0]
    h = x.reshape(t, d)
    k_sh = v_sh = bias = None
    n_att_heads = q_w.shape[2] // LANES

    for layer in range(depth):
        if layer < n_a:
            i = layer
            d_inner = ssm_out_w.shape[1]
            n_ssm_heads = ssm_dt_bias.shape[1]
            conv_dim = ssm_conv_w.shape[2]
            w_in = ssm_in_w[i].astype(BF16)
            w_z = w_in[:, :d_inner]
            w_xbc = w_in[:, d_inner:d_inner + conv_dim]
            w_dt = jnp.pad(w_in[:, d_inner + conv_dim:], ((0, 0), (0, LANES - n_ssm_heads)))
            z, xbc, dt = _norm_proj(h, ssm_ln_g[i], [w_z, w_xbc, w_dt], [F32, F32, F32], "ssm_in_proj")
            pad_h = ((0, 0), (0, LANES - n_ssm_heads))
            y = _ssd(z, xbc, dt, ssm_conv_w[i], ssm_conv_b[i].reshape(1, -1),
                     jnp.pad(ssm_dt_bias[i].reshape(1, -1), pad_h),
                     jnp.pad(ssm_a_log[i].reshape(1, -1), pad_h),
                     jnp.repeat(ssm_d[i], SSM_HEAD_DIM).reshape(1, -1),
                     ssm_norm_g[i].reshape(1, -1), batch, seq)
            h = _proj_residual(y, ssm_out_w[i].astype(BF16), h, "ssm_out_proj")
        else:
            j = layer - n_a
            if j == 0:
                k_sh, v_sh = _kv_proj(h, kv_ln_g, kv_w.astype(BF16),
                                      jnp.tile(k_norm_g, 2).reshape(1, LANES), n_att_heads)
                bias = _bias_tiles(rel_bias, n_att_heads, ATT_TILE)
            lam_init = 0.8 - 0.6 * math.exp(-0.3 * layer)
            o = _attention(h, attn_ln_g[j], q_w[j].astype(BF16),
                           jnp.tile(q_norm_g[j], 2).reshape(1, LANES), lam_vecs[j], subln_g[j],
                           k_sh, v_sh, bias, lam_init, batch, seq)
            h = _proj_residual(o, attn_out_w[j].astype(BF16), h, "attn_out_proj")
        h = _ffn(h, ffn_ln_g[layer], ffn_up_w[layer].astype(BF16), ffn_conv_w[layer],
                 ffn_conv_b[layer], ffn_down_w[layer].astype(BF16), seq, f"conv_ffn_{layer}")
    return h.reshape(batch, seq, d)
```

```python
import functools
import math

import numpy as np
import jax
import jax.numpy as jnp
from jax import lax
from jax.experimental import pallas as pl
from jax.experimental.pallas import tpu as pltpu

F32 = jnp.float32
BF16 = jnp.bfloat16

EPS = 1e-6
NEG = -1e30

SSM_GROUPS = 4
SSM_STATE = 128
SSM_HEAD_DIM = 64
SSM_CHUNK = 128
ATT_HEAD_DIM = 64
NUM_BUCKETS = 32
MAX_DISTANCE = 128
LANES = 128
SUBLANES = 8
VMEM_LIMIT = 56 * 1024 * 1024

ROW_TILE = 256
ATT_TILE = 256
ATT_ROWS = 128
ATT_HEADS_PER_STEP = 4


def _params(*sem):
    return pltpu.CompilerParams(dimension_semantics=sem, vmem_limit_bytes=VMEM_LIMIT)


def _const_spec(shape):
    nd = len(shape)
    return pl.BlockSpec(shape, lambda *_: (0,) * nd, pipeline_mode=pl.Buffered(1))


def _rms(x, g):
    ms = jnp.mean(x * x, axis=-1, keepdims=True)
    return x * lax.rsqrt(ms + EPS) * g


def _silu(x):
    return x * (1.0 / (1.0 + jnp.exp(-x)))


def _split3(v):
    hi = v.astype(BF16)
    r1 = v - hi.astype(F32)
    mid = r1.astype(BF16)
    lo = (r1 - mid.astype(F32)).astype(BF16)
    return hi, mid, lo


def _dot(a, b):
    return jnp.dot(a, b, preferred_element_type=F32)


def _dot_nt(a, b):
    return lax.dot_general(a, b, (((1,), (1,)), ((), ())), preferred_element_type=F32)


def _norm_proj_kernel(n_out, x_ref, g_ref, *refs):
    w_refs, o_refs = refs[:n_out], refs[n_out:]
    xn = _rms(x_ref[...], g_ref[...]).astype(BF16)
    for w_ref, o_ref in zip(w_refs, o_refs):
        o_ref[...] = _dot(xn, w_ref[...]).astype(o_ref.dtype)


def _norm_proj(x, g, ws, out_dtypes, name):
    t, d = x.shape
    tm = ROW_TILE
    in_specs = [pl.BlockSpec((tm, d), lambda i: (i, 0)), _const_spec((1, d))]
    in_specs += [_const_spec(w.shape) for w in ws]
    out_specs = [pl.BlockSpec((tm, w.shape[1]), lambda i: (i, 0)) for w in ws]
    out_shape = [jax.ShapeDtypeStruct((t, w.shape[1]), dt) for w, dt in zip(ws, out_dtypes)]
    return pl.pallas_call(
        functools.partial(_norm_proj_kernel, len(ws)),
        grid=(t // tm,), in_specs=in_specs, out_specs=out_specs, out_shape=out_shape,
        compiler_params=_params("arbitrary"), name=name,
    )(x, g.reshape(1, d), *ws)


def _proj_residual_kernel(y_ref, w_ref, r_ref, o_ref):
    o_ref[...] = r_ref[...] + _dot(y_ref[...], w_ref[...])


def _proj_residual(y, w, res, name):
    t, k = y.shape
    d = w.shape[1]
    tm = ROW_TILE
    return pl.pallas_call(
        _proj_residual_kernel, grid=(t // tm,),
        in_specs=[pl.BlockSpec((tm, k), lambda i: (i, 0)), _const_spec(w.shape),
                  pl.BlockSpec((tm, d), lambda i: (i, 0))],
        out_specs=pl.BlockSpec((tm, d), lambda i: (i, 0)),
        out_shape=jax.ShapeDtypeStruct((t, d), F32),
        compiler_params=_params("arbitrary"), name=name,
    )(y, w, res)


def _ssd_kernel(z_ref, xbc_ref, dt_ref, cw_ref, cb_ref, dtb_ref, alog_ref, dskip_ref, ng_ref,
                o_ref, cbuf, state, ybuf):
    l = SSM_CHUNK
    d_inner = z_ref.shape[1]
    n_pairs = d_inner // LANES
    pairs_per_group = n_pairs // SSM_GROUPS
    gn = SSM_GROUPS * SSM_STATE
    c = pl.program_id(1)
    halo = SUBLANES

    @pl.when(c == 0)
    def _():
        cbuf[0:halo, :] = jnp.zeros((halo, cbuf.shape[1]), F32)
        state[...] = jnp.zeros(state.shape, F32)

    @pl.when(c > 0)
    def _():
        cbuf[0:halo, :] = cbuf[l:l + halo, :]

    u = xbc_ref[...]
    cbuf[halo:halo + l, :] = u
    k_w = cw_ref.shape[0]
    conv = cw_ref[k_w - 1:k_w, :] * u + cb_ref[...]
    for s in range(1, k_w):
        conv = conv + cw_ref[k_w - 1 - s:k_w - s, :] * cbuf[halo - s:halo - s + l, :]
    act = _silu(conv)
    xs = act[:, :d_inner]
    bm = act[:, d_inner:d_inner + gn].astype(BF16)
    cm_f = act[:, d_inner + gn:]

    pre = dt_ref[...] + dtb_ref[...]
    dtv = jnp.maximum(pre, 0.0) + jnp.log(1.0 + jnp.exp(-jnp.abs(pre)))
    a = dtv * (-jnp.exp(alog_ref[...]))

    row = lax.broadcasted_iota(jnp.int32, (l, l), 0)
    col = lax.broadcasted_iota(jnp.int32, (l, l), 1)
    causal = row >= col
    tri = jnp.where(causal, 1.0, 0.0).astype(BF16)
    a_hi, a_mid, a_lo = _split3(a)
    cs = _dot(tri, a_hi) + _dot(tri, a_mid) + _dot(tri, a_lo)

    cs_t = cs.T
    dt_t = dtv.T
    tot = cs_t[:, l - 1:l]
    w_t = dt_t * jnp.exp(tot - cs_t)
    g_t = jnp.broadcast_to(jnp.exp(tot), (LANES, l))
    xs_t = xs.T
    lane = lax.broadcasted_iota(jnp.int32, (l, LANES), 1)
    sub = lax.broadcasted_iota(jnp.int32, (LANES, l), 0)
    lane_lo = lane < SSM_HEAD_DIM
    sub_lo = sub < SSM_HEAD_DIM

    for g in range(SSM_GROUPS):
        b_g = bm[:, g * SSM_STATE:(g + 1) * SSM_STATE]
        c_gf = cm_f[:, g * SSM_STATE:(g + 1) * SSM_STATE]
        cb = _dot_nt(c_gf.astype(BF16), b_g)
        for q in range(pairs_per_group):
            pi = g * pairs_per_group + q
            xs_pair = xs[:, pi * LANES:(pi + 1) * LANES]
            s_pair = state[pi]
            s_pair_b = s_pair.astype(BF16)
            y_pair = jnp.zeros((l, LANES), F32)
            for hh in range(2):
                h = 2 * pi + hh
                colb = jnp.broadcast_to(cs[:, h:h + 1], (l, l))
                rowb = jnp.broadcast_to(cs_t[h:h + 1, :], (l, l))
                dtrow = jnp.broadcast_to(dt_t[h:h + 1, :], (l, l))
                dec = jnp.exp(jnp.where(causal, colb - rowb, NEG))
                part1 = (dec * (cb * dtrow)).astype(BF16)
                part2 = (jnp.exp(colb) * c_gf).astype(BF16)
                mask = lane_lo if hh == 0 else jnp.logical_not(lane_lo)
                xs_m = jnp.where(mask, xs_pair, 0.0).astype(BF16)
                y_pair = y_pair + _dot(part1, xs_m)
                y_pair = y_pair + jnp.where(mask, _dot_nt(part2, s_pair_b), 0.0)
            ybuf[:, pi * LANES:(pi + 1) * LANES] = y_pair
            h0 = 2 * pi
            w2 = jnp.where(sub_lo, jnp.broadcast_to(w_t[h0:h0 + 1, :], (LANES, l)),
                           jnp.broadcast_to(w_t[h0 + 1:h0 + 2, :], (LANES, l)))
            g2 = jnp.where(sub_lo, jnp.broadcast_to(g_t[h0:h0 + 1, :], (LANES, l)),
                           jnp.broadcast_to(g_t[h0 + 1:h0 + 2, :], (LANES, l)))
            xst = xs_t[pi * LANES:(pi + 1) * LANES, :]
            state[pi] = g2 * s_pair + _dot((xst * w2).astype(BF16), b_g)

    y = ybuf[...] + xs * dskip_ref[...]
    y = y * _silu(z_ref[...])
    gw = d_inner // SSM_GROUPS
    for g in range(SSM_GROUPS):
        sl = slice(g * gw, (g + 1) * gw)
        o_ref[:, sl] = _rms(y[:, sl], ng_ref[:, sl]).astype(o_ref.dtype)


def _ssd(z, xbc, dt, cw, cb, dtb, alog, dskip, ng, batch, seq):
    t, d_inner = z.shape
    conv_dim = xbc.shape[1]
    l = SSM_CHUNK
    nc = seq // l
    row = lambda b, c: (b * nc + c, 0)
    return pl.pallas_call(
        _ssd_kernel, grid=(batch, nc),
        in_specs=[pl.BlockSpec((l, d_inner), row), pl.BlockSpec((l, conv_dim), row),
                  pl.BlockSpec((l, LANES), row),
                  _const_spec(cw.shape), _const_spec(cb.shape), _const_spec(dtb.shape),
                  _const_spec(alog.shape), _const_spec(dskip.shape), _const_spec(ng.shape)],
        out_specs=pl.BlockSpec((l, d_inner), row),
        out_shape=jax.ShapeDtypeStruct((t, d_inner), BF16),
        scratch_shapes=[pltpu.VMEM((l + SUBLANES, conv_dim), F32),
                        pltpu.VMEM((d_inner // LANES, LANES, SSM_STATE), F32),
                        pltpu.VMEM((l, d_inner), F32)],
        compiler_params=_params("arbitrary", "arbitrary"), name="ssd",
    )(z, xbc, dt, cw, cb, dtb, alog, dskip, ng)


def _ffn_kernel(tiles_per_seq, x_ref, g_ref, wup_ref, cw_ref, cb_ref, wdn_ref, o_ref, cbuf):
    tm = x_ref.shape[0]
    f = wdn_ref.shape[0]
    halo = SUBLANES
    i = pl.program_id(0)
    first = (i % tiles_per_seq) == 0

    @pl.when(first)
    def _():
        cbuf[0:halo, :] = jnp.zeros((halo, cbuf.shape[1]), F32)

    @pl.when(jnp.logical_not(first))
    def _():
        cbuf[0:halo, :] = cbuf[tm:tm + halo, :]

    x = x_ref[...]
    xn = _rms(x, g_ref[...]).astype(BF16)
    u = _dot(xn, wup_ref[...])
    cbuf[halo:halo + tm, :] = u
    k_w = cw_ref.shape[0]
    conv = cw_ref[k_w - 1:k_w, :] * u + cb_ref[...]
    for s in range(1, k_w):
        conv = conv + cw_ref[k_w - 1 - s:k_w - s, :] * cbuf[halo - s:halo - s + tm, :]
    hid = (_silu(conv[:, :f]) * conv[:, f:]).astype(BF16)
    o_ref[...] = x + _dot(hid, wdn_ref[...])


def _ffn(x, g, wup, cw, cb, wdn, seq, name):
    t, d = x.shape
    tm = ROW_TILE
    return pl.pallas_call(
        functools.partial(_ffn_kernel, seq // tm), grid=(t // tm,),
        in_specs=[pl.BlockSpec((tm, d), lambda i: (i, 0)), _const_spec((1, d)),
                  _const_spec(wup.shape), _const_spec(cw.shape), _const_spec((1, cb.shape[0])),
                  _const_spec(wdn.shape)],
        out_specs=pl.BlockSpec((tm, d), lambda i: (i, 0)),
        out_shape=jax.ShapeDtypeStruct((t, d), F32),
        scratch_shapes=[pltpu.VMEM((tm + SUBLANES, wup.shape[1]), F32)],
        compiler_params=_params("arbitrary"), name=name,
    )(x, g.reshape(1, d), wup, cw, cb.reshape(1, -1), wdn)


def _pair_norm(blk, g2):
    lane = lax.broadcasted_iota(jnp.int32, blk.shape, 1)
    lo = lane < ATT_HEAD_DIM
    sq = blk * blk
    s_lo = jnp.sum(jnp.where(lo, sq, 0.0), axis=-1, keepdims=True)
    s_hi = jnp.sum(jnp.where(lo, 0.0, sq), axis=-1, keepdims=True)
    inv = 1.0 / ATT_HEAD_DIM
    rs = jnp.where(lo, lax.rsqrt(s_lo * inv + EPS), lax.rsqrt(s_hi * inv + EPS))
    return blk * rs * g2


def _kv_kernel(x_ref, g_ref, w_ref, kg_ref, k_ref, v_ref):
    n_heads = k_ref.shape[0]
    xn = _rms(x_ref[...], g_ref[...]).astype(BF16)
    kv = _dot(xn, w_ref[...])
    kd = n_heads * LANES
    for h in range(n_heads):
        sl = slice(h * LANES, (h + 1) * LANES)
        k_ref[h] = _pair_norm(kv[:, sl], kg_ref[...]).astype(k_ref.dtype)
        v_ref[h] = kv[:, kd + h * LANES:kd + (h + 1) * LANES].astype(v_ref.dtype)


def _kv_proj(x, g, w, kg2, n_heads):
    t, d = x.shape
    tm = ROW_TILE
    hm = jax.ShapeDtypeStruct((n_heads, t, LANES), BF16)
    hspec = pl.BlockSpec((n_heads, tm, LANES), lambda i: (0, i, 0))
    return pl.pallas_call(
        _kv_kernel, grid=(t // tm,),
        in_specs=[pl.BlockSpec((tm, d), lambda i: (i, 0)), _const_spec((1, d)),
                  _const_spec(w.shape), _const_spec((1, LANES))],
        out_specs=[hspec, hspec], out_shape=[hm, hm],
        compiler_params=_params("arbitrary"), name="kv_proj",
    )(x, g.reshape(1, d), w, kg2)


def _bucket_thresholds():
    n = np.arange(1, 4 * MAX_DISTANCE, dtype=np.int64)
    max_exact = NUM_BUCKETS // 2
    nf = n.astype(np.float32)
    large = max_exact + (np.log(nf / np.float32(max_exact)) / np.float32(math.log(MAX_DISTANCE / max_exact))
                         * np.float32(NUM_BUCKETS - max_exact)).astype(np.int32)
    large = np.minimum(large, NUM_BUCKETS - 1)
    bucket = np.where(n < max_exact, n, large)
    return [int(n[np.argmax(bucket >= b)]) for b in range(max_exact + 1, NUM_BUCKETS)]


def _bias_kernel(thresholds, rb_ref, o_ref):
    t = o_ref.shape[2]
    h = pl.program_id(0)
    max_exact = NUM_BUCKETS // 2
    row = lax.broadcasted_iota(jnp.int32, (t, t), 0)
    col = lax.broadcasted_iota(jnp.int32, (t, t), 1)
    far = rb_ref[NUM_BUCKETS - 1, h]
    for kind in range(2):
        d = row - col + kind * t
        n = jnp.maximum(d, 0)
        bucket = jnp.minimum(n, max_exact)
        for thr in thresholds:
            bucket = bucket + jnp.where(n >= thr, 1, 0)
        val = jnp.zeros((t, t), F32)
        for b in range(NUM_BUCKETS):
            val = jnp.where(bucket == b, rb_ref[b, h] - far, val)
        if kind == 0:
            val = jnp.where(d >= 0, val, NEG)
        o_ref[0, kind] = val


def _bias_tiles(rel_bias, n_heads, t):
    return pl.pallas_call(
        functools.partial(_bias_kernel, _bucket_thresholds()), grid=(n_heads,),
        in_specs=[pl.BlockSpec(memory_space=pltpu.SMEM)],
        out_specs=pl.BlockSpec((1, 2, t, t), lambda h: (h, 0, 0, 0)),
        out_shape=jax.ShapeDtypeStruct((n_heads, 2, t, t), F32),
        compiler_params=_params("arbitrary"), name="rel_bias_tiles",
    )(rel_bias)


def _attn_kernel(lam_init, x_ref, g_ref, qw_ref, qg_ref, lv_ref, sg_ref, k_ref, v_ref, bias_ref,
                 o_ref, q_scr, m_scr, acc_scr):
    t = x_ref.shape[0]
    n_heads = k_ref.shape[0]
    qi = pl.program_id(1)

    xn = _rms(x_ref[...], g_ref[...]).astype(BF16)
    q = _dot(xn, qw_ref[...])
    lane = lax.broadcasted_iota(jnp.int32, (t, LANES), 1)
    lo = lane < ATT_HEAD_DIM
    scale = ATT_HEAD_DIM ** -0.5
    for h in range(n_heads):
        qn = _pair_norm(q[:, h * LANES:(h + 1) * LANES], qg_ref[...]) * scale
        q_scr[h, 0:t, :] = jnp.where(lo, qn, 0.0).astype(BF16)
        q_scr[h, t:2 * t, :] = jnp.where(lo, 0.0, qn).astype(BF16)

    lv = lv_ref[...]
    lam = (jnp.exp(jnp.sum(lv[0:1] * lv[1:2], axis=-1, keepdims=True))
           - jnp.exp(jnp.sum(lv[2:3] * lv[3:4], axis=-1, keepdims=True)) + lam_init)

    ones = jnp.ones((t, LANES), BF16)

    def group_body(hg, carry):
        m_scr[...] = jnp.full(m_scr.shape, NEG, F32)
        acc_scr[...] = jnp.zeros(acc_scr.shape, F32)

        def step(j, kind):
            start = pl.multiple_of(j * t, t)
            for hh in range(ATT_HEADS_PER_STEP):
                h = hg * ATT_HEADS_PER_STEP + hh
                kt = k_ref[h, pl.ds(start, t), :]
                vx = jnp.concatenate([v_ref[h, pl.ds(start, t), :], ones], axis=1)
                for r0 in range(0, 2 * t, ATT_ROWS):
                    rows = slice(r0, r0 + ATT_ROWS)
                    s = _dot_nt(q_scr[h, rows, :], kt)
                    if kind is not None:
                        b0 = r0 % t
                        s = s + bias_ref[h, kind, b0:b0 + ATT_ROWS, :]
                    m_prev = m_scr[hh, rows, :]
                    m_new = jnp.maximum(m_prev, jnp.max(s, axis=-1, keepdims=True))
                    alpha = jnp.exp(m_prev - m_new)
                    p = jnp.exp(s - jnp.concatenate([m_new] * (t // LANES), axis=1)).astype(BF16)
                    acc_scr[hh, rows, :] = (jnp.concatenate([alpha, alpha], axis=1) * acc_scr[hh, rows, :]
                                            + _dot(p, vx))
                    m_scr[hh, rows, :] = m_new

        def far_step(j, c):
            step(j, None)
            return c

        lax.fori_loop(0, jnp.maximum(qi - 1, 0), far_step, 0)

        @pl.when(qi >= 1)
        def _():
            step(qi - 1, 1)

        step(qi, 0)

        for hh in range(ATT_HEADS_PER_STEP):
            h = hg * ATT_HEADS_PER_STEP + hh
            acc = acc_scr[hh]
            o = acc[:, :LANES] / acc[:, LANES:]
            d = o[:t] - lam * o[t:]
            d = _rms(d, sg_ref[...]) * (1.0 - lam_init)
            o_ref[:, pl.ds(pl.multiple_of(h * LANES, LANES), LANES)] = d.astype(o_ref.dtype)
        return carry

    lax.fori_loop(0, n_heads // ATT_HEADS_PER_STEP, group_body, 0)


def _attention(x, g, qw, qg2, lam_vecs, sg, k, v, bias, lam_init, batch, seq):
    tt, d = x.shape
    n_heads = k.shape[0]
    t = ATT_TILE
    nq = seq // t
    kd = qw.shape[1]
    kv_spec = pl.BlockSpec((n_heads, seq, LANES), lambda b, i: (0, b, 0))
    return pl.pallas_call(
        functools.partial(_attn_kernel, lam_init), grid=(batch, nq),
        in_specs=[pl.BlockSpec((t, d), lambda b, i: (b * nq + i, 0)), _const_spec((1, d)),
                  _const_spec(qw.shape), _const_spec((1, LANES)), _const_spec(lam_vecs.shape),
                  _const_spec((1, LANES)), kv_spec, kv_spec, _const_spec(bias.shape)],
        out_specs=pl.BlockSpec((t, kd), lambda b, i: (b * nq + i, 0)),
        out_shape=jax.ShapeDtypeStruct((tt, kd), BF16),
        scratch_shapes=[pltpu.VMEM((n_heads, 2 * t, LANES), BF16),
                        pltpu.VMEM((ATT_HEADS_PER_STEP, 2 * t, LANES), F32),
                        pltpu.VMEM((ATT_HEADS_PER_STEP, 2 * t, 2 * LANES), F32)],
        compiler_params=_params("arbitrary", "arbitrary"), name="diff_attention",
    )(x, g.reshape(1, d), qw, qg2, lam_vecs, sg.reshape(1, LANES), k, v, bias)


def kernel(x, ssm_ln_g, ssm_in_w, ssm_conv_w, ssm_conv_b, ssm_dt_bias, ssm_a_log, ssm_d, ssm_norm_g, ssm_out_w, kv_ln_g, kv_w, k_norm_g, rel_bias, attn_ln_g, q_w, q_norm_g, lam_vecs, subln_g, attn_out_w, ffn_ln_g, ffn_up_w, ffn_conv_w, ffn_conv_b, ffn_down_w):
    batch, seq, d = x.shape
    t = batch * seq
    n_a = ssm_in_w.shape[0]
    depth = ffn_up_w.shape[0]
    h = x.reshape(t, d)
    k_sh = v_sh = bias = None
    n_att_heads = q_w.shape[2] // LANES

    for layer in range(depth):
        if layer < n_a:
            i = layer
            d_inner = ssm_out_w.shape[1]
            n_ssm_heads = ssm_dt_bias.shape[1]
            conv_dim = ssm_conv_w.shape[2]
            w_in = ssm_in_w[i].astype(BF16)
            w_z = w_in[:, :d_inner]
            w_xbc = w_in[:, d_inner:d_inner + conv_dim]
            w_dt = jnp.pad(w_in[:, d_inner + conv_dim:], ((0, 0), (0, LANES - n_ssm_heads)))
            z, xbc, dt = _norm_proj(h, ssm_ln_g[i], [w_z, w_xbc, w_dt], [F32, F32, F32], "ssm_in_proj")
            pad_h = ((0, 0), (0, LANES - n_ssm_heads))
            y = _ssd(z, xbc, dt, ssm_conv_w[i], ssm_conv_b[i].reshape(1, -1),
                     jnp.pad(ssm_dt_bias[i].reshape(1, -1), pad_h),
                     jnp.pad(ssm_a_log[i].reshape(1, -1), pad_h),
                     jnp.repeat(ssm_d[i], SSM_HEAD_DIM).reshape(1, -1),
                     ssm_norm_g[i].reshape(1, -1), batch, seq)
            h = _proj_residual(y, ssm_out_w[i].astype(BF16), h, "ssm_out_proj")
        else:
            j = layer - n_a
            if j == 0:
                k_sh, v_sh = _kv_proj(h, kv_ln_g, kv_w.astype(BF16),
                                      jnp.tile(k_norm_g, 2).reshape(1, LANES), n_att_heads)
                bias = _bias_tiles(rel_bias, n_att_heads, ATT_TILE)
            lam_init = 0.8 - 0.6 * math.exp(-0.3 * layer)
            o = _attention(h, attn_ln_g[j], q_w[j].astype(BF16),
                           jnp.tile(q_norm_g[j], 2).reshape(1, LANES), lam_vecs[j], subln_g[j],
                           k_sh, v_sh, bias, lam_init, batch, seq)
            h = _proj_residual(o, attn_out_w[j].astype(BF16), h, "attn_out_proj")
        h = _ffn(h, ffn_ln_g[layer], ffn_up_w[layer].astype(BF16), ffn_conv_w[layer],
                 ffn_conv_b[layer], ffn_down_w[layer].astype(BF16), seq, f"conv_ffn_{layer}")
    return h.reshape(batch, seq, d)
```

```python
import functools
import math

import numpy as np
import jax
import jax.numpy as jnp
from jax import lax
from jax.experimental import pallas as pl
from jax.experimental.pallas import tpu as pltpu

F32 = jnp.float32
BF16 = jnp.bfloat16

EPS = 1e-6
NEG = -1e30
LOG2E = math.log2(math.e)

SSM_GROUPS = 4
SSM_STATE = 128
SSM_HEAD_DIM = 64
SSM_CHUNK = 128
ATT_HEAD_DIM = 64
NUM_BUCKETS = 32
MAX_DISTANCE = 128
LANES = 128
SUBLANES = 8
VMEM_LIMIT = 56 * 1024 * 1024

ROW_TILE = 256
FFN_TILE = 512
ATT_TILE = 256
ATT_ROWS = 128
ATT_HEADS_PER_STEP = 8


def _params(*sem):
    return pltpu.CompilerParams(dimension_semantics=sem, vmem_limit_bytes=VMEM_LIMIT)


def _const_spec(shape):
    nd = len(shape)
    return pl.BlockSpec(shape, lambda *_: (0,) * nd, pipeline_mode=pl.Buffered(1))


def _rms(x, g):
    ms = jnp.mean(x * x, axis=-1, keepdims=True)
    return x * lax.rsqrt(ms + EPS) * g


def _silu(x):
    h = 0.5 * x
    return h * jnp.tanh(h) + h


def _split3(v):
    hi = v.astype(BF16)
    r1 = v - hi.astype(F32)
    mid = r1.astype(BF16)
    lo = (r1 - mid.astype(F32)).astype(BF16)
    return hi, mid, lo


def _dot(a, b):
    return jnp.dot(a, b, preferred_element_type=F32)


def _dot_nt(a, b):
    return lax.dot_general(a, b, (((1,), (1,)), ((), ())), preferred_element_type=F32)


def _norm_proj_kernel(n_out, x_ref, g_ref, *refs):
    w_refs, o_refs = refs[:n_out], refs[n_out:]
    xn = _rms(x_ref[...], g_ref[...]).astype(BF16)
    for w_ref, o_ref in zip(w_refs, o_refs):
        o_ref[...] = _dot(xn, w_ref[...]).astype(o_ref.dtype)


def _norm_proj(x, g, ws, out_dtypes, name):
    t, d = x.shape
    tm = ROW_TILE
    in_specs = [pl.BlockSpec((tm, d), lambda i: (i, 0)), _const_spec((1, d))]
    in_specs += [_const_spec(w.shape) for w in ws]
    out_specs = [pl.BlockSpec((tm, w.shape[1]), lambda i: (i, 0)) for w in ws]
    out_shape = [jax.ShapeDtypeStruct((t, w.shape[1]), dt) for w, dt in zip(ws, out_dtypes)]
    return pl.pallas_call(
        functools.partial(_norm_proj_kernel, len(ws)),
        grid=(t // tm,), in_specs=in_specs, out_specs=out_specs, out_shape=out_shape,
        compiler_params=_params("arbitrary"), name=name,
    )(x, g.reshape(1, d), *ws)


def _proj_residual_kernel(y_ref, w_ref, r_ref, o_ref):
    o_ref[...] = r_ref[...] + _dot(y_ref[...], w_ref[...])


def _proj_residual(y, w, res, name):
    t, k = y.shape
    d = w.shape[1]
    tm = ROW_TILE
    return pl.pallas_call(
        _proj_residual_kernel, grid=(t // tm,),
        in_specs=[pl.BlockSpec((tm, k), lambda i: (i, 0)), _const_spec(w.shape),
                  pl.BlockSpec((tm, d), lambda i: (i, 0))],
        out_specs=pl.BlockSpec((tm, d), lambda i: (i, 0)),
        out_shape=jax.ShapeDtypeStruct((t, d), F32),
        compiler_params=_params("arbitrary"), name=name,
    )(y, w, res)


def _ssd_kernel(z_ref, xbc_ref, dt_ref, cw_ref, cb_ref, dtb_ref, alog_ref, dskip_ref, ng_ref,
                o_ref, cbuf, state, ybuf):
    l = SSM_CHUNK
    d_inner = z_ref.shape[1]
    n_pairs = d_inner // LANES
    pairs_per_group = n_pairs // SSM_GROUPS
    gn = SSM_GROUPS * SSM_STATE
    c = pl.program_id(1)
    halo = SUBLANES

    @pl.when(c == 0)
    def _():
        cbuf[0:halo, :] = jnp.zeros((halo, cbuf.shape[1]), F32)
        state[...] = jnp.zeros(state.shape, F32)

    @pl.when(c > 0)
    def _():
        cbuf[0:halo, :] = cbuf[l:l + halo, :]

    u = xbc_ref[...]
    cbuf[halo:halo + l, :] = u
    k_w = cw_ref.shape[0]
    conv = cw_ref[k_w - 1:k_w, :] * u + cb_ref[...]
    for s in range(1, k_w):
        conv = conv + cw_ref[k_w - 1 - s:k_w - s, :] * cbuf[halo - s:halo - s + l, :]
    act = _silu(conv)
    xs = act[:, :d_inner]
    bm = act[:, d_inner:d_inner + gn].astype(BF16)
    cm_f = act[:, d_inner + gn:]

    pre = dt_ref[...] + dtb_ref[...]
    dtv = jnp.maximum(pre, 0.0) + jnp.log(1.0 + jnp.exp(-jnp.abs(pre)))
    a = dtv * (-jnp.exp(alog_ref[...]))

    row = lax.broadcasted_iota(jnp.int32, (l, l), 0)
    col = lax.broadcasted_iota(jnp.int32, (l, l), 1)
    causal = row >= col
    tri = jnp.where(causal, 1.0, 0.0).astype(BF16)
    a_hi, a_mid, a_lo = _split3(a)
    cs = _dot(tri, a_hi) + _dot(tri, a_mid) + _dot(tri, a_lo)

    cs_t = cs.T
    dt_t = dtv.T
    tot = cs_t[:, l - 1:l]
    w_t = dt_t * jnp.exp(tot - cs_t)
    g_t = jnp.broadcast_to(jnp.exp(tot), (LANES, l))
    xs_t = xs.T
    lane = lax.broadcasted_iota(jnp.int32, (l, LANES), 1)
    sub = lax.broadcasted_iota(jnp.int32, (LANES, l), 0)
    lane_lo = lane < SSM_HEAD_DIM
    sub_lo = sub < SSM_HEAD_DIM

    for g in range(SSM_GROUPS):
        b_g = bm[:, g * SSM_STATE:(g + 1) * SSM_STATE]
        c_gf = cm_f[:, g * SSM_STATE:(g + 1) * SSM_STATE]
        cb = _dot_nt(c_gf.astype(BF16), b_g)
        for q in range(pairs_per_group):
            pi = g * pairs_per_group + q
            xs_pair = xs[:, pi * LANES:(pi + 1) * LANES]
            s_pair = state[pi]
            s_pair_b = s_pair.astype(BF16)
            y_pair = jnp.zeros((l, LANES), F32)
            for hh in range(2):
                h = 2 * pi + hh
                colb = jnp.broadcast_to(cs[:, h:h + 1], (l, l))
                rowb = jnp.broadcast_to(cs_t[h:h + 1, :], (l, l))
                dtrow = jnp.broadcast_to(dt_t[h:h + 1, :], (l, l))
                dec = jnp.exp(jnp.where(causal, colb - rowb, NEG))
                part1 = (dec * (cb * dtrow)).astype(BF16)
                part2 = (jnp.exp(colb) * c_gf).astype(BF16)
                mask = lane_lo if hh == 0 else jnp.logical_not(lane_lo)
                xs_m = jnp.where(mask, xs_pair, 0.0).astype(BF16)
                y_pair = y_pair + _dot(part1, xs_m)
                y_pair = y_pair + jnp.where(mask, _dot_nt(part2, s_pair_b), 0.0)
            ybuf[:, pi * LANES:(pi + 1) * LANES] = y_pair
            h0 = 2 * pi
            w2 = jnp.where(sub_lo, jnp.broadcast_to(w_t[h0:h0 + 1, :], (LANES, l)),
                           jnp.broadcast_to(w_t[h0 + 1:h0 + 2, :], (LANES, l)))
            g2 = jnp.where(sub_lo, jnp.broadcast_to(g_t[h0:h0 + 1, :], (LANES, l)),
                           jnp.broadcast_to(g_t[h0 + 1:h0 + 2, :], (LANES, l)))
            xst = xs_t[pi * LANES:(pi + 1) * LANES, :]
            state[pi] = g2 * s_pair + _dot((xst * w2).astype(BF16), b_g)

    y = ybuf[...] + xs * dskip_ref[...]
    y = y * _silu(z_ref[...])
    gw = d_inner // SSM_GROUPS
    for g in range(SSM_GROUPS):
        sl = slice(g * gw, (g + 1) * gw)
        o_ref[:, sl] = _rms(y[:, sl], ng_ref[:, sl]).astype(o_ref.dtype)


def _ssd(z, xbc, dt, cw, cb, dtb, alog, dskip, ng, batch, seq):
    t, d_inner = z.shape
    conv_dim = xbc.shape[1]
    l = SSM_CHUNK
    nc = seq // l
    row = lambda b, c: (b * nc + c, 0)
    return pl.pallas_call(
        _ssd_kernel, grid=(batch, nc),
        in_specs=[pl.BlockSpec((l, d_inner), row), pl.BlockSpec((l, conv_dim), row),
                  pl.BlockSpec((l, LANES), row),
                  _const_spec(cw.shape), _const_spec(cb.shape), _const_spec(dtb.shape),
                  _const_spec(alog.shape), _const_spec(dskip.shape), _const_spec(ng.shape)],
        out_specs=pl.BlockSpec((l, d_inner), row),
        out_shape=jax.ShapeDtypeStruct((t, d_inner), BF16),
        scratch_shapes=[pltpu.VMEM((l + SUBLANES, conv_dim), F32),
                        pltpu.VMEM((d_inner // LANES, LANES, SSM_STATE), F32),
                        pltpu.VMEM((l, d_inner), F32)],
        compiler_params=_params("arbitrary", "arbitrary"), name="ssd",
    )(z, xbc, dt, cw, cb, dtb, alog, dskip, ng)


def _ffn_kernel(tiles_per_seq, x_ref, g_ref, wup_ref, cw_ref, cb_ref, wdn_ref, o_ref, cbuf):
    tm = x_ref.shape[0]
    f = wdn_ref.shape[0]
    halo = SUBLANES
    i = pl.program_id(0)
    first = (i % tiles_per_seq) == 0

    @pl.when(first)
    def _():
        cbuf[0:halo, :] = jnp.zeros((halo, cbuf.shape[1]), F32)

    @pl.when(jnp.logical_not(first))
    def _():
        cbuf[0:halo, :] = cbuf[tm:tm + halo, :]

    x = x_ref[...]
    xn = _rms(x, g_ref[...]).astype(BF16)
    u = _dot(xn, wup_ref[...])
    cbuf[halo:halo + tm, :] = u
    k_w = cw_ref.shape[0]
    conv = cw_ref[k_w - 1:k_w, :] * u + cb_ref[...]
    for s in range(1, k_w):
        conv = conv + cw_ref[k_w - 1 - s:k_w - s, :] * cbuf[halo - s:halo - s + tm, :]
    hid = (_silu(conv[:, :f]) * conv[:, f:]).astype(BF16)
    o_ref[...] = x + _dot(hid, wdn_ref[...])


def _ffn(x, g, wup, cw, cb, wdn, seq, name):
    t, d = x.shape
    tm = FFN_TILE
    return pl.pallas_call(
        functools.partial(_ffn_kernel, seq // tm), grid=(t // tm,),
        in_specs=[pl.BlockSpec((tm, d), lambda i: (i, 0)), _const_spec((1, d)),
                  _const_spec(wup.shape), _const_spec(cw.shape), _const_spec((1, cb.shape[0])),
                  _const_spec(wdn.shape)],
        out_specs=pl.BlockSpec((tm, d), lambda i: (i, 0)),
        out_shape=jax.ShapeDtypeStruct((t, d), F32),
        scratch_shapes=[pltpu.VMEM((tm + SUBLANES, wup.shape[1]), F32)],
        compiler_params=_params("arbitrary"), name=name,
    )(x, g.reshape(1, d), wup, cw, cb.reshape(1, -1), wdn)


def _pair_norm(blk, g2):
    lane = lax.broadcasted_iota(jnp.int32, blk.shape, 1)
    lo = lane < ATT_HEAD_DIM
    sq = blk * blk
    s_lo = jnp.sum(jnp.where(lo, sq, 0.0), axis=-1, keepdims=True)
    s_hi = jnp.sum(jnp.where(lo, 0.0, sq), axis=-1, keepdims=True)
    inv = 1.0 / ATT_HEAD_DIM
    rs = jnp.where(lo, lax.rsqrt(s_lo * inv + EPS), lax.rsqrt(s_hi * inv + EPS))
    return blk * rs * g2


def _kv_kernel(x_ref, g_ref, w_ref, kg_ref, k_ref, v_ref):
    n_heads = k_ref.shape[0]
    xn = _rms(x_ref[...], g_ref[...]).astype(BF16)
    kv = _dot(xn, w_ref[...])
    kd = n_heads * LANES
    for h in range(n_heads):
        sl = slice(h * LANES, (h + 1) * LANES)
        k_ref[h] = _pair_norm(kv[:, sl], kg_ref[...]).astype(k_ref.dtype)
        v_ref[h] = kv[:, kd + h * LANES:kd + (h + 1) * LANES].astype(v_ref.dtype)


def _kv_proj(x, g, w, kg2, n_heads):
    t, d = x.shape
    tm = ROW_TILE
    hm = jax.ShapeDtypeStruct((n_heads, t, LANES), BF16)
    hspec = pl.BlockSpec((n_heads, tm, LANES), lambda i: (0, i, 0))
    return pl.pallas_call(
        _kv_kernel, grid=(t // tm,),
        in_specs=[pl.BlockSpec((tm, d), lambda i: (i, 0)), _const_spec((1, d)),
                  _const_spec(w.shape), _const_spec((1, LANES))],
        out_specs=[hspec, hspec], out_shape=[hm, hm],
        compiler_params=_params("arbitrary"), name="kv_proj",
    )(x, g.reshape(1, d), w, kg2)


def _bucket_thresholds():
    n = np.arange(1, 4 * MAX_DISTANCE, dtype=np.int64)
    max_exact = NUM_BUCKETS // 2
    nf = n.astype(np.float32)
    large = max_exact + (np.log(nf / np.float32(max_exact)) / np.float32(math.log(MAX_DISTANCE / max_exact))
                         * np.float32(NUM_BUCKETS - max_exact)).astype(np.int32)
    large = np.minimum(large, NUM_BUCKETS - 1)
    bucket = np.where(n < max_exact, n, large)
    return [int(n[np.argmax(bucket >= b)]) for b in range(max_exact + 1, NUM_BUCKETS)]


def _bias_kernel(thresholds, rb_ref, o_ref):
    t = o_ref.shape[2]
    h = pl.program_id(0)
    max_exact = NUM_BUCKETS // 2
    row = lax.broadcasted_iota(jnp.int32, (t, t), 0)
    col = lax.broadcasted_iota(jnp.int32, (t, t), 1)
    far = rb_ref[NUM_BUCKETS - 1, h]
    for kind in range(2):
        d = row - col + kind * t
        n = jnp.maximum(d, 0)
        bucket = jnp.minimum(n, max_exact)
        for thr in thresholds:
            bucket = bucket + jnp.where(n >= thr, 1, 0)
        val = jnp.zeros((t, t), F32)
        for b in range(NUM_BUCKETS):
            val = jnp.where(bucket == b, (rb_ref[b, h] - far) * LOG2E, val)
        if kind == 0:
            val = jnp.where(d >= 0, val, NEG)
        o_ref[0, kind] = val


def _bias_tiles(rel_bias, n_heads, t):
    return pl.pallas_call(
        functools.partial(_bias_kernel, _bucket_thresholds()), grid=(n_heads,),
        in_specs=[pl.BlockSpec(memory_space=pltpu.SMEM)],
        out_specs=pl.BlockSpec((1, 2, t, t), lambda h: (h, 0, 0, 0)),
        out_shape=jax.ShapeDtypeStruct((n_heads, 2, t, t), F32),
        compiler_params=_params("arbitrary"), name="rel_bias_tiles",
    )(rel_bias)


def _attn_kernel(lam_init, x_ref, g_ref, qw_ref, qg_ref, lv_ref, sg_ref, k_ref, v_ref, bias_ref,
                 o_ref, q_scr, m_scr, acc_scr):
    t = x_ref.shape[0]
    n_heads = k_ref.shape[0]
    qi = pl.program_id(1)

    xn = _rms(x_ref[...], g_ref[...]).astype(BF16)
    q = _dot(xn, qw_ref[...])
    lane = lax.broadcasted_iota(jnp.int32, (t, LANES), 1)
    lo = lane < ATT_HEAD_DIM
    scale = ATT_HEAD_DIM ** -0.5 * LOG2E
    for h in range(n_heads):
        qn = _pair_norm(q[:, h * LANES:(h + 1) * LANES], qg_ref[...]) * scale
        q_scr[h, 0:t, :] = jnp.where(lo, qn, 0.0).astype(BF16)
        q_scr[h, t:2 * t, :] = jnp.where(lo, 0.0, qn).astype(BF16)

    lv = lv_ref[...]
    lam = (jnp.exp(jnp.sum(lv[0:1] * lv[1:2], axis=-1, keepdims=True))
           - jnp.exp(jnp.sum(lv[2:3] * lv[3:4], axis=-1, keepdims=True)) + lam_init)

    ones = jnp.ones((t, LANES), BF16)

    def group_body(hg, carry):
        m_scr[...] = jnp.full(m_scr.shape, NEG, F32)
        acc_scr[...] = jnp.zeros(acc_scr.shape, F32)

        def step(j, kind):
            start = pl.multiple_of(j * t, t)
            for hh in range(ATT_HEADS_PER_STEP):
                h = hg * ATT_HEADS_PER_STEP + hh
                kt = k_ref[h, pl.ds(start, t), :]
                vx = jnp.concatenate([v_ref[h, pl.ds(start, t), :], ones], axis=1)
                for r0 in range(0, 2 * t, ATT_ROWS):
                    rows = slice(r0, r0 + ATT_ROWS)
                    s = _dot_nt(q_scr[h, rows, :], kt)
                    if kind is not None:
                        b0 = r0 % t
                        s = s + bias_ref[h, kind, b0:b0 + ATT_ROWS, :]
                    m_prev = m_scr[hh, rows, :]
                    m_new = jnp.maximum(m_prev, jnp.max(s, axis=-1, keepdims=True))
                    alpha = jnp.exp2(m_prev - m_new)
                    p = jnp.exp2(s - jnp.concatenate([m_new] * (t // LANES), axis=1)).astype(BF16)
                    acc_scr[hh, rows, :] = (jnp.concatenate([alpha, alpha], axis=1) * acc_scr[hh, rows, :]
                                            + _dot(p, vx))
                    m_scr[hh, rows, :] = m_new

        def far_step(j, c):
            step(j, None)
            return c

        lax.fori_loop(0, jnp.maximum(qi - 1, 0), far_step, 0)

        @pl.when(qi >= 1)
        def _():
            step(qi - 1, 1)

        step(qi, 0)

        for hh in range(ATT_HEADS_PER_STEP):
            h = hg * ATT_HEADS_PER_STEP + hh
            acc = acc_scr[hh]
            o = acc[:, :LANES] / acc[:, LANES:]
            d = o[:t] - lam * o[t:]
            d = _rms(d, sg_ref[...]) * (1.0 - lam_init)
            o_ref[:, pl.ds(pl.multiple_of(h * LANES, LANES), LANES)] = d.astype(o_ref.dtype)
        return carry

    lax.fori_loop(0, n_heads // ATT_HEADS_PER_STEP, group_body, 0)


def _attention(x, g, qw, qg2, lam_vecs, sg, k, v, bias, lam_init, batch, seq):
    tt, d = x.shape
    n_heads = k.shape[0]
    t = ATT_TILE
    nq = seq // t
    kd = qw.shape[1]
    kv_spec = pl.BlockSpec((n_heads, seq, LANES), lambda b, i: (0, b, 0))
    return pl.pallas_call(
        functools.partial(_attn_kernel, lam_init), grid=(batch, nq),
        in_specs=[pl.BlockSpec((t, d), lambda b, i: (b * nq + i, 0)), _const_spec((1, d)),
                  _const_spec(qw.shape), _const_spec((1, LANES)), _const_spec(lam_vecs.shape),
                  _const_spec((1, LANES)), kv_spec, kv_spec, _const_spec(bias.shape)],
        out_specs=pl.BlockSpec((t, kd), lambda b, i: (b * nq + i, 0)),
        out_shape=jax.ShapeDtypeStruct((tt, kd), BF16),
        scratch_shapes=[pltpu.VMEM((n_heads, 2 * t, LANES), BF16),
                        pltpu.VMEM((ATT_HEADS_PER_STEP, 2 * t, LANES), F32),
                        pltpu.VMEM((ATT_HEADS_PER_STEP, 2 * t, 2 * LANES), F32)],
        compiler_params=_params("arbitrary", "arbitrary"), name="diff_attention",
    )(x, g.reshape(1, d), qw, qg2, lam_vecs, sg.reshape(1, LANES), k, v, bias)


def kernel(x, ssm_ln_g, ssm_in_w, ssm_conv_w, ssm_conv_b, ssm_dt_bias, ssm_a_log, ssm_d, ssm_norm_g, ssm_out_w, kv_ln_g, kv_w, k_norm_g, rel_bias, attn_ln_g, q_w, q_norm_g, lam_vecs, subln_g, attn_out_w, ffn_ln_g, ffn_up_w, ffn_conv_w, ffn_conv_b, ffn_down_w):
    batch, seq, d = x.shape
    t = batch * seq
    n_a = ssm_in_w.shape[0]
    depth = ffn_up_w.shape[0]
    h = x.reshape(t, d)
    k_sh = v_sh = bias = None
    n_att_heads = q_w.shape[2] // LANES

    for layer in range(depth):
        if layer < n_a:
            i = layer
            d_inner = ssm_out_w.shape[1]
            n_ssm_heads = ssm_dt_bias.shape[1]
            conv_dim = ssm_conv_w.shape[2]
            w_in = ssm_in_w[i].astype(BF16)
            w_z = w_in[:, :d_inner]
            w_xbc = w_in[:, d_inner:d_inner + conv_dim]
            w_dt = jnp.pad(w_in[:, d_inner + conv_dim:], ((0, 0), (0, LANES - n_ssm_heads)))
            z, xbc, dt = _norm_proj(h, ssm_ln_g[i], [w_z, w_xbc, w_dt], [F32, F32, F32], "ssm_in_proj")
            pad_h = ((0, 0), (0, LANES - n_ssm_heads))
            y = _ssd(z, xbc, dt, ssm_conv_w[i], ssm_conv_b[i].reshape(1, -1),
                     jnp.pad(ssm_dt_bias[i].reshape(1, -1), pad_h),
                     jnp.pad(ssm_a_log[i].reshape(1, -1), pad_h),
                     jnp.repeat(ssm_d[i], SSM_HEAD_DIM).reshape(1, -1),
                     ssm_norm_g[i].reshape(1, -1), batch, seq)
            h = _proj_residual(y, ssm_out_w[i].astype(BF16), h, "ssm_out_proj")
        else:
            j = layer - n_a
            if j == 0:
                k_sh, v_sh = _kv_proj(h, kv_ln_g, kv_w.astype(BF16),
                                      jnp.tile(k_norm_g, 2).reshape(1, LANES), n_att_heads)
                bias = _bias_tiles(rel_bias, n_att_heads, ATT_TILE)
            lam_init = 0.8 - 0.6 * math.exp(-0.3 * layer)
            o = _attention(h, attn_ln_g[j], q_w[j].astype(BF16),
                           jnp.tile(q_norm_g[j], 2).reshape(1, LANES), lam_vecs[j], subln_g[j],
                           k_sh, v_sh, bias, lam_init, batch, seq)
            h = _proj_residual(o, attn_out_w[j].astype(BF16), h, "attn_out_proj")
        h = _ffn(h, ffn_ln_g[layer], ffn_up_w[layer].astype(BF16), ffn_conv_w[layer],
                 ffn_conv_b[layer], ffn_down_w[layer].astype(BF16), seq, f"conv_ffn_{layer}")
    return h.reshape(batch, seq, d)
```

```python
import functools
import math

import numpy as np
import jax
import jax.numpy as jnp
from jax import lax
from jax.experimental import pallas as pl
from jax.experimental.pallas import tpu as pltpu

F32 = jnp.float32
BF16 = jnp.bfloat16

EPS = 1e-6
NEG = -1e30
LOG2E = math.log2(math.e)

SSM_GROUPS = 4
SSM_STATE = 128
SSM_HEAD_DIM = 64
SSM_CHUNK = 128
ATT_HEAD_DIM = 64
NUM_BUCKETS = 32
MAX_DISTANCE = 128
LANES = 128
SUBLANES = 8
VMEM_LIMIT = 56 * 1024 * 1024

ROW_TILE = 512
FFN_TILE = 512
ATT_TILE = 256
ATT_ROWS = 128
ATT_HEADS_PER_STEP = 8


def _params(*sem):
    return pltpu.CompilerParams(dimension_semantics=sem, vmem_limit_bytes=VMEM_LIMIT)


def _const_spec(shape):
    nd = len(shape)
    return pl.BlockSpec(shape, lambda *_: (0,) * nd, pipeline_mode=pl.Buffered(1))


def _rms(x, g):
    ms = jnp.mean(x * x, axis=-1, keepdims=True)
    return x * lax.rsqrt(ms + EPS) * g


def _silu(x):
    h = 0.5 * x
    return h * jnp.tanh(h) + h


def _split3(v):
    hi = v.astype(BF16)
    r1 = v - hi.astype(F32)
    mid = r1.astype(BF16)
    lo = (r1 - mid.astype(F32)).astype(BF16)
    return hi, mid, lo


def _dot(a, b):
    return jnp.dot(a, b, preferred_element_type=F32)


def _dot_nt(a, b):
    return lax.dot_general(a, b, (((1,), (1,)), ((), ())), preferred_element_type=F32)


def _norm_proj_kernel(n_out, x_ref, g_ref, *refs):
    w_refs, o_refs = refs[:n_out], refs[n_out:]
    xn = _rms(x_ref[...], g_ref[...]).astype(BF16)
    for w_ref, o_ref in zip(w_refs, o_refs):
        o_ref[...] = _dot(xn, w_ref[...]).astype(o_ref.dtype)


def _norm_proj(x, g, ws, out_dtypes, name):
    t, d = x.shape
    tm = ROW_TILE
    in_specs = [pl.BlockSpec((tm, d), lambda i: (i, 0)), _const_spec((1, d))]
    in_specs += [_const_spec(w.shape) for w in ws]
    out_specs = [pl.BlockSpec((tm, w.shape[1]), lambda i: (i, 0)) for w in ws]
    out_shape = [jax.ShapeDtypeStruct((t, w.shape[1]), dt) for w, dt in zip(ws, out_dtypes)]
    return pl.pallas_call(
        functools.partial(_norm_proj_kernel, len(ws)),
        grid=(t // tm,), in_specs=in_specs, out_specs=out_specs, out_shape=out_shape,
        compiler_params=_params("arbitrary"), name=name,
    )(x, g.reshape(1, d), *ws)


def _ssd_kernel(z_ref, xbc_ref, dt_ref, cw_ref, cb_ref, dtb_ref, alog_ref, dskip_ref, ng_ref,
                o_ref, cbuf, state, ybuf):
    l = SSM_CHUNK
    d_inner = z_ref.shape[1]
    n_pairs = d_inner // LANES
    pairs_per_group = n_pairs // SSM_GROUPS
    gn = SSM_GROUPS * SSM_STATE
    c = pl.program_id(1)
    halo = SUBLANES

    @pl.when(c == 0)
    def _():
        cbuf[0:halo, :] = jnp.zeros((halo, cbuf.shape[1]), F32)
        state[...] = jnp.zeros(state.shape, F32)

    @pl.when(c > 0)
    def _():
        cbuf[0:halo, :] = cbuf[l:l + halo, :]

    u = xbc_ref[...]
    cbuf[halo:halo + l, :] = u
    k_w = cw_ref.shape[0]
    conv = cw_ref[k_w - 1:k_w, :] * u + cb_ref[...]
    for s in range(1, k_w):
        conv = conv + cw_ref[k_w - 1 - s:k_w - s, :] * cbuf[halo - s:halo - s + l, :]
    act = _silu(conv)
    xs = act[:, :d_inner]
    bm = act[:, d_inner:d_inner + gn].astype(BF16)
    cm_f = act[:, d_inner + gn:]

    pre = dt_ref[...] + dtb_ref[...]
    dtv = jnp.maximum(pre, 0.0) + jnp.log(1.0 + jnp.exp(-jnp.abs(pre)))
    a = dtv * (-jnp.exp(alog_ref[...]))

    row = lax.broadcasted_iota(jnp.int32, (l, l), 0)
    col = lax.broadcasted_iota(jnp.int32, (l, l), 1)
    causal = row >= col
    tri = jnp.where(causal, 1.0, 0.0).astype(BF16)
    a_hi, a_mid, a_lo = _split3(a)
    cs = _dot(tri, a_hi) + _dot(tri, a_mid) + _dot(tri, a_lo)

    cs_t = cs.T
    dt_t = dtv.T
    tot = cs_t[:, l - 1:l]
    w_t = dt_t * jnp.exp(tot - cs_t)
    g_t = jnp.broadcast_to(jnp.exp(tot), (LANES, l))
    xs_t = xs.T
    lane = lax.broadcasted_iota(jnp.int32, (l, LANES), 1)
    sub = lax.broadcasted_iota(jnp.int32, (LANES, l), 0)
    lane_lo = lane < SSM_HEAD_DIM
    sub_lo = sub < SSM_HEAD_DIM

    for g in range(SSM_GROUPS):
        b_g = bm[:, g * SSM_STATE:(g + 1) * SSM_STATE]
        c_gf = cm_f[:, g * SSM_STATE:(g + 1) * SSM_STATE]
        cb = _dot_nt(c_gf.astype(BF16), b_g)
        for q in range(pairs_per_group):
            pi = g * pairs_per_group + q
            xs_pair = xs[:, pi * LANES:(pi + 1) * LANES]
            s_pair = state[pi]
            s_pair_b = s_pair.astype(BF16)
            y_pair = jnp.zeros((l, LANES), F32)
            for hh in range(2):
                h = 2 * pi + hh
                colb = jnp.broadcast_to(cs[:, h:h + 1], (l, l))
                rowb = jnp.broadcast_to(cs_t[h:h + 1, :], (l, l))
                dtrow = jnp.broadcast_to(dt_t[h:h + 1, :], (l, l))
                dec = jnp.exp(jnp.where(causal, colb - rowb, NEG))
                part1 = (dec * (cb * dtrow)).astype(BF16)
                part2 = (jnp.exp(colb) * c_gf).astype(BF16)
                mask = lane_lo if hh == 0 else jnp.logical_not(lane_lo)
                xs_m = jnp.where(mask, xs_pair, 0.0).astype(BF16)
                y_pair = y_pair + _dot(part1, xs_m)
                y_pair = y_pair + jnp.where(mask, _dot_nt(part2, s_pair_b), 0.0)
            ybuf[:, pi * LANES:(pi + 1) * LANES] = y_pair
            h0 = 2 * pi
            w2 = jnp.where(sub_lo, jnp.broadcast_to(w_t[h0:h0 + 1, :], (LANES, l)),
                           jnp.broadcast_to(w_t[h0 + 1:h0 + 2, :], (LANES, l)))
            g2 = jnp.where(sub_lo, jnp.broadcast_to(g_t[h0:h0 + 1, :], (LANES, l)),
                           jnp.broadcast_to(g_t[h0 + 1:h0 + 2, :], (LANES, l)))
            xst = xs_t[pi * LANES:(pi + 1) * LANES, :]
            state[pi] = g2 * s_pair + _dot((xst * w2).astype(BF16), b_g)

    y = ybuf[...] + xs * dskip_ref[...]
    y = y * _silu(z_ref[...])
    gw = d_inner // SSM_GROUPS
    for g in range(SSM_GROUPS):
        sl = slice(g * gw, (g + 1) * gw)
        o_ref[:, sl] = _rms(y[:, sl], ng_ref[:, sl]).astype(o_ref.dtype)


def _ssd(z, xbc, dt, cw, cb, dtb, alog, dskip, ng, batch, seq):
    t, d_inner = z.shape
    conv_dim = xbc.shape[1]
    l = SSM_CHUNK
    nc = seq // l
    row = lambda b, c: (b * nc + c, 0)
    return pl.pallas_call(
        _ssd_kernel, grid=(batch, nc),
        in_specs=[pl.BlockSpec((l, d_inner), row), pl.BlockSpec((l, conv_dim), row),
                  pl.BlockSpec((l, LANES), row),
                  _const_spec(cw.shape), _const_spec(cb.shape), _const_spec(dtb.shape),
                  _const_spec(alog.shape), _const_spec(dskip.shape), _const_spec(ng.shape)],
        out_specs=pl.BlockSpec((l, d_inner), row),
        out_shape=jax.ShapeDtypeStruct((t, d_inner), BF16),
        scratch_shapes=[pltpu.VMEM((l + SUBLANES, conv_dim), F32),
                        pltpu.VMEM((d_inner // LANES, LANES, SSM_STATE), F32),
                        pltpu.VMEM((l, d_inner), F32)],
        compiler_params=_params("arbitrary", "arbitrary"), name="ssd",
    )(z, xbc, dt, cw, cb, dtb, alog, dskip, ng)


def _ffn_kernel(tiles_per_seq, y_ref, wmix_ref, x_ref, g_ref, wup_ref, cw_ref, cb_ref, wdn_ref,
                o_ref, cbuf):
    tm = x_ref.shape[0]
    f = wdn_ref.shape[0]
    halo = SUBLANES
    i = pl.program_id(0)
    first = (i % tiles_per_seq) == 0

    @pl.when(first)
    def _():
        cbuf[0:halo, :] = jnp.zeros((halo, cbuf.shape[1]), F32)

    @pl.when(jnp.logical_not(first))
    def _():
        cbuf[0:halo, :] = cbuf[tm:tm + halo, :]

    x = x_ref[...] + _dot(y_ref[...], wmix_ref[...])
    xn = _rms(x, g_ref[...]).astype(BF16)
    u = _dot(xn, wup_ref[...])
    cbuf[halo:halo + tm, :] = u
    k_w = cw_ref.shape[0]
    conv = cw_ref[k_w - 1:k_w, :] * u + cb_ref[...]
    for s in range(1, k_w):
        conv = conv + cw_ref[k_w - 1 - s:k_w - s, :] * cbuf[halo - s:halo - s + tm, :]
    hid = (_silu(conv[:, :f]) * conv[:, f:]).astype(BF16)
    o_ref[...] = x + _dot(hid, wdn_ref[...])


def _ffn(y, wmix, x, g, wup, cw, cb, wdn, seq, name):
    t, d = x.shape
    tm = FFN_TILE
    return pl.pallas_call(
        functools.partial(_ffn_kernel, seq // tm), grid=(t // tm,),
        in_specs=[pl.BlockSpec((tm, y.shape[1]), lambda i: (i, 0)), _const_spec(wmix.shape),
                  pl.BlockSpec((tm, d), lambda i: (i, 0)), _const_spec((1, d)),
                  _const_spec(wup.shape), _const_spec(cw.shape), _const_spec((1, cb.shape[0])),
                  _const_spec(wdn.shape)],
        out_specs=pl.BlockSpec((tm, d), lambda i: (i, 0)),
        out_shape=jax.ShapeDtypeStruct((t, d), F32),
        scratch_shapes=[pltpu.VMEM((tm + SUBLANES, wup.shape[1]), F32)],
        compiler_params=_params("arbitrary"), name=name,
    )(y, wmix, x, g.reshape(1, d), wup, cw, cb.reshape(1, -1), wdn)


def _pair_norm(blk, g2):
    lane = lax.broadcasted_iota(jnp.int32, blk.shape, 1)
    lo = lane < ATT_HEAD_DIM
    sq = blk * blk
    s_lo = jnp.sum(jnp.where(lo, sq, 0.0), axis=-1, keepdims=True)
    s_hi = jnp.sum(jnp.where(lo, 0.0, sq), axis=-1, keepdims=True)
    inv = 1.0 / ATT_HEAD_DIM
    rs = jnp.where(lo, lax.rsqrt(s_lo * inv + EPS), lax.rsqrt(s_hi * inv + EPS))
    return blk * rs * g2


def _kv_kernel(x_ref, g_ref, w_ref, kg_ref, k_ref, v_ref):
    n_heads = k_ref.shape[0]
    xn = _rms(x_ref[...], g_ref[...]).astype(BF16)
    kv = _dot(xn, w_ref[...])
    kd = n_heads * LANES
    for h in range(n_heads):
        sl = slice(h * LANES, (h + 1) * LANES)
        k_ref[h] = _pair_norm(kv[:, sl], kg_ref[...]).astype(k_ref.dtype)
        v_ref[h] = kv[:, kd + h * LANES:kd + (h + 1) * LANES].astype(v_ref.dtype)


def _kv_proj(x, g, w, kg2, n_heads):
    t, d = x.shape
    tm = ROW_TILE
    hm = jax.ShapeDtypeStruct((n_heads, t, LANES), BF16)
    hspec = pl.BlockSpec((n_heads, tm, LANES), lambda i: (0, i, 0))
    return pl.pallas_call(
        _kv_kernel, grid=(t // tm,),
        in_specs=[pl.BlockSpec((tm, d), lambda i: (i, 0)), _const_spec((1, d)),
                  _const_spec(w.shape), _const_spec((1, LANES))],
        out_specs=[hspec, hspec], out_shape=[hm, hm],
        compiler_params=_params("arbitrary"), name="kv_proj",
    )(x, g.reshape(1, d), w, kg2)


def _bucket_thresholds():
    n = np.arange(1, 4 * MAX_DISTANCE, dtype=np.int64)
    max_exact = NUM_BUCKETS // 2
    nf = n.astype(np.float32)
    large = max_exact + (np.log(nf / np.float32(max_exact)) / np.float32(math.log(MAX_DISTANCE / max_exact))
                         * np.float32(NUM_BUCKETS - max_exact)).astype(np.int32)
    large = np.minimum(large, NUM_BUCKETS - 1)
    bucket = np.where(n < max_exact, n, large)
    return [int(n[np.argmax(bucket >= b)]) for b in range(max_exact + 1, NUM_BUCKETS)]


def _bias_kernel(thresholds, rb_ref, o_ref):
    t = o_ref.shape[2]
    h = pl.program_id(0)
    max_exact = NUM_BUCKETS // 2
    row = lax.broadcasted_iota(jnp.int32, (t, t), 0)
    col = lax.broadcasted_iota(jnp.int32, (t, t), 1)
    far = rb_ref[NUM_BUCKETS - 1, h]
    for kind in range(2):
        d = row - col + kind * t
        n = jnp.maximum(d, 0)
        bucket = jnp.minimum(n, max_exact)
        for thr in thresholds:
            bucket = bucket + jnp.where(n >= thr, 1, 0)
        val = jnp.zeros((t, t), F32)
        for b in range(NUM_BUCKETS):
            val = jnp.where(bucket == b, (rb_ref[b, h] - far) * LOG2E, val)
        if kind == 0:
            val = jnp.where(d >= 0, val, NEG)
        o_ref[0, kind] = val


def _bias_tiles(rel_bias, n_heads, t):
    return pl.pallas_call(
        functools.partial(_bias_kernel, _bucket_thresholds()), grid=(n_heads,),
        in_specs=[pl.BlockSpec(memory_space=pltpu.SMEM)],
        out_specs=pl.BlockSpec((1, 2, t, t), lambda h: (h, 0, 0, 0)),
        out_shape=jax.ShapeDtypeStruct((n_heads, 2, t, t), F32),
        compiler_params=_params("arbitrary"), name="rel_bias_tiles",
    )(rel_bias)


def _attn_kernel(lam_init, x_ref, g_ref, qw_ref, qg_ref, lv_ref, sg_ref, k_ref, v_ref, bias_ref,
                 o_ref, q_scr, m_scr, acc_scr):
    t = x_ref.shape[0]
    n_heads = k_ref.shape[0]
    qi = pl.program_id(1)

    xn = _rms(x_ref[...], g_ref[...]).astype(BF16)
    q = _dot(xn, qw_ref[...])
    lane = lax.broadcasted_iota(jnp.int32, (t, LANES), 1)
    lo = lane < ATT_HEAD_DIM
    scale = ATT_HEAD_DIM ** -0.5 * LOG2E
    for h in range(n_heads):
        qn = _pair_norm(q[:, h * LANES:(h + 1) * LANES], qg_ref[...]) * scale
        q_scr[h, 0:t, :] = jnp.where(lo, qn, 0.0).astype(BF16)
        q_scr[h, t:2 * t, :] = jnp.where(lo, 0.0, qn).astype(BF16)

    lv = lv_ref[...]
    lam = (jnp.exp(jnp.sum(lv[0:1] * lv[1:2], axis=-1, keepdims=True))
           - jnp.exp(jnp.sum(lv[2:3] * lv[3:4], axis=-1, keepdims=True)) + lam_init)

    ones = jnp.ones((t, LANES), BF16)

    def group_body(hg, carry):
        m_scr[...] = jnp.full(m_scr.shape, NEG, F32)
        acc_scr[...] = jnp.zeros(acc_scr.shape, F32)

        def step(j, kind):
            start = pl.multiple_of(j * t, t)
            for hh in range(ATT_HEADS_PER_STEP):
                h = hg * ATT_HEADS_PER_STEP + hh
                kt = k_ref[h, pl.ds(start, t), :]
                vx = jnp.concatenate([v_ref[h, pl.ds(start, t), :], ones], axis=1)
                for r0 in range(0, 2 * t, ATT_ROWS):
                    rows = slice(r0, r0 + ATT_ROWS)
                    s = _dot_nt(q_scr[h, rows, :], kt)
                    if kind is not None:
                        b0 = r0 % t
                        s = s + bias_ref[h, kind, b0:b0 + ATT_ROWS, :]
                    m_prev = m_scr[hh, rows, :]
                    m_new = jnp.maximum(m_prev, jnp.max(s, axis=-1, keepdims=True))
                    alpha = jnp.exp2(m_prev - m_new)
                    p = jnp.exp2(s - jnp.concatenate([m_new] * (t // LANES), axis=1)).astype(BF16)
                    acc_scr[hh, rows, :] = (jnp.concatenate([alpha, alpha], axis=1) * acc_scr[hh, rows, :]
                                            + _dot(p, vx))
                    m_scr[hh, rows, :] = m_new

        def far_step(j, c):
            step(j, None)
            return c

        lax.fori_loop(0, jnp.maximum(qi - 1, 0), far_step, 0)

        @pl.when(qi >= 1)
        def _():
            step(qi - 1, 1)

        step(qi, 0)

        for hh in range(ATT_HEADS_PER_STEP):
            h = hg * ATT_HEADS_PER_STEP + hh
            acc = acc_scr[hh]
            o = acc[:, :LANES] / acc[:, LANES:]
            d = o[:t] - lam * o[t:]
            d = _rms(d, sg_ref[...]) * (1.0 - lam_init)
            o_ref[:, pl.ds(pl.multiple_of(h * LANES, LANES), LANES)] = d.astype(o_ref.dtype)
        return carry

    lax.fori_loop(0, n_heads // ATT_HEADS_PER_STEP, group_body, 0)


def _attention(x, g, qw, qg2, lam_vecs, sg, k, v, bias, lam_init, batch, seq):
    tt, d = x.shape
    n_heads = k.shape[0]
    t = ATT_TILE
    nq = seq // t
    kd = qw.shape[1]
    kv_spec = pl.BlockSpec((n_heads, seq, LANES), lambda b, i: (0, b, 0))
    return pl.pallas_call(
        functools.partial(_attn_kernel, lam_init), grid=(batch, nq),
        in_specs=[pl.BlockSpec((t, d), lambda b, i: (b * nq + i, 0)), _const_spec((1, d)),
                  _const_spec(qw.shape), _const_spec((1, LANES)), _const_spec(lam_vecs.shape),
                  _const_spec((1, LANES)), kv_spec, kv_spec, _const_spec(bias.shape)],
        out_specs=pl.BlockSpec((t, kd), lambda b, i: (b * nq + i, 0)),
        out_shape=jax.ShapeDtypeStruct((tt, kd), BF16),
        scratch_shapes=[pltpu.VMEM((n_heads, 2 * t, LANES), BF16),
                        pltpu.VMEM((ATT_HEADS_PER_STEP, 2 * t, LANES), F32),
                        pltpu.VMEM((ATT_HEADS_PER_STEP, 2 * t, 2 * LANES), F32)],
        compiler_params=_params("arbitrary", "arbitrary"), name="diff_attention",
    )(x, g.reshape(1, d), qw, qg2, lam_vecs, sg.reshape(1, LANES), k, v, bias)


def kernel(x, ssm_ln_g, ssm_in_w, ssm_conv_w, ssm_conv_b, ssm_dt_bias, ssm_a_log, ssm_d, ssm_norm_g, ssm_out_w, kv_ln_g, kv_w, k_norm_g, rel_bias, attn_ln_g, q_w, q_norm_g, lam_vecs, subln_g, attn_out_w, ffn_ln_g, ffn_up_w, ffn_conv_w, ffn_conv_b, ffn_down_w):
    batch, seq, d = x.shape
    t = batch * seq
    n_a = ssm_in_w.shape[0]
    depth = ffn_up_w.shape[0]
    h = x.reshape(t, d)
    k_sh = v_sh = bias = None
    n_att_heads = q_w.shape[2] // LANES

    for layer in range(depth):
        if layer < n_a:
            i = layer
            d_inner = ssm_out_w.shape[1]
            n_ssm_heads = ssm_dt_bias.shape[1]
            conv_dim = ssm_conv_w.shape[2]
            w_in = ssm_in_w[i].astype(BF16)
            w_z = w_in[:, :d_inner]
            w_xbc = w_in[:, d_inner:d_inner + conv_dim]
            w_dt = jnp.pad(w_in[:, d_inner + conv_dim:], ((0, 0), (0, LANES - n_ssm_heads)))
            z, xbc, dt = _norm_proj(h, ssm_ln_g[i], [w_z, w_xbc, w_dt], [F32, F32, F32], "ssm_in_proj")
            pad_h = ((0, 0), (0, LANES - n_ssm_heads))
            mix = _ssd(z, xbc, dt, ssm_conv_w[i], ssm_conv_b[i].reshape(1, -1),
                       jnp.pad(ssm_dt_bias[i].reshape(1, -1), pad_h),
                       jnp.pad(ssm_a_log[i].reshape(1, -1), pad_h),
                       jnp.repeat(ssm_d[i], SSM_HEAD_DIM).reshape(1, -1),
                       ssm_norm_g[i].reshape(1, -1), batch, seq)
            w_mix = ssm_out_w[i].astype(BF16)
        else:
            j = layer - n_a
            if j == 0:
                k_sh, v_sh = _kv_proj(h, kv_ln_g, kv_w.astype(BF16),
                                      jnp.tile(k_norm_g, 2).reshape(1, LANES), n_att_heads)
                bias = _bias_tiles(rel_bias, n_att_heads, ATT_TILE)
            lam_init = 0.8 - 0.6 * math.exp(-0.3 * layer)
            mix = _attention(h, attn_ln_g[j], q_w[j].astype(BF16),
                             jnp.tile(q_norm_g[j], 2).reshape(1, LANES), lam_vecs[j], subln_g[j],
                             k_sh, v_sh, bias, lam_init, batch, seq)
            w_mix = attn_out_w[j].astype(BF16)
        h = _ffn(mix, w_mix, h, ffn_ln_g[layer], ffn_up_w[layer].astype(BF16), ffn_conv_w[layer],
                 ffn_conv_b[layer], ffn_down_w[layer].astype(BF16), seq, f"conv_ffn_{layer}")
    return h.reshape(batch, seq, d)
```

```python
import functools
import math

import numpy as np
import jax
import jax.numpy as jnp
from jax import lax
from jax.experimental import pallas as pl
from jax.experimental.pallas import tpu as pltpu

F32 = jnp.float32
BF16 = jnp.bfloat16

EPS = 1e-6
NEG = -1e30
LOG2E = math.log2(math.e)

SSM_GROUPS = 4
SSM_STATE = 128
SSM_HEAD_DIM = 64
SSM_CHUNK = 128
ATT_HEAD_DIM = 64
NUM_BUCKETS = 32
MAX_DISTANCE = 128
LANES = 128
SUBLANES = 8
VMEM_LIMIT = 56 * 1024 * 1024

ROW_TILE = 512
FFN_TILE = 512
ATT_TILE = 256
ATT_ROWS = 128


def _params(*sem):
    return pltpu.CompilerParams(dimension_semantics=sem, vmem_limit_bytes=VMEM_LIMIT)


def _const_spec(shape):
    nd = len(shape)
    return pl.BlockSpec(shape, lambda *_: (0,) * nd, pipeline_mode=pl.Buffered(1))


def _rms(x, g):
    ms = jnp.mean(x * x, axis=-1, keepdims=True)
    return x * lax.rsqrt(ms + EPS) * g


def _silu(x):
    h = 0.5 * x
    return h * jnp.tanh(h) + h


def _split3(v):
    hi = v.astype(BF16)
    r1 = v - hi.astype(F32)
    mid = r1.astype(BF16)
    lo = (r1 - mid.astype(F32)).astype(BF16)
    return hi, mid, lo


def _dot(a, b):
    return jnp.dot(a, b, preferred_element_type=F32)


def _dot_nt(a, b):
    return lax.dot_general(a, b, (((1,), (1,)), ((), ())), preferred_element_type=F32)


def _norm_proj_kernel(n_out, x_ref, g_ref, *refs):
    w_refs, o_refs = refs[:n_out], refs[n_out:]
    xn = _interleave_rows(_rms(x_ref[...], g_ref[...]).astype(BF16))
    for w_ref, o_ref in zip(w_refs, o_refs):
        o_ref[...] = _dot(xn, w_ref[...]).astype(o_ref.dtype)


def _norm_proj(x, g, ws, out_dtypes, name):
    t, d = x.shape
    tm = ROW_TILE
    in_specs = [pl.BlockSpec((tm, d), lambda i: (i, 0)), _const_spec((1, d))]
    in_specs += [_const_spec(w.shape) for w in ws]
    out_specs = [pl.BlockSpec((tm, w.shape[1]), lambda i: (i, 0)) for w in ws]
    out_shape = [jax.ShapeDtypeStruct((t, w.shape[1]), dt) for w, dt in zip(ws, out_dtypes)]
    return pl.pallas_call(
        functools.partial(_norm_proj_kernel, len(ws)),
        grid=(t // tm,), in_specs=in_specs, out_specs=out_specs, out_shape=out_shape,
        compiler_params=_params("arbitrary"), name=name,
    )(x, g.reshape(1, d), *ws)


GROUPS_PER_CHUNK = SSM_CHUNK // SUBLANES


def _chunk_time(r):
    return r // SUBLANES + GROUPS_PER_CHUNK * (r % SUBLANES)


def _interleave_rows(v, inverse=False):
    l = SSM_CHUNK
    r = lax.broadcasted_iota(jnp.int32, (l, l), 0)
    c = lax.broadcasted_iota(jnp.int32, (l, l), 1)
    hit = (r == _chunk_time(c)) if inverse else (c == _chunk_time(r))
    perm = jnp.where(hit, 1.0, 0.0).astype(BF16)
    out = [_dot(perm, v[b:b + l, :]).astype(BF16) for b in range(0, v.shape[0], l)]
    return out[0] if len(out) == 1 else jnp.concatenate(out, axis=0)


def _ssd_kernel(z_ref, xbc_ref, dt_ref, cw_ref, cb_ref, dtb_ref, alog_ref, dskip_ref, ng_ref,
                o_ref, ctail, state, yt_scr):
    l = SSM_CHUNK
    d_inner = z_ref.shape[1]
    n_heads = d_inner // SSM_HEAD_DIM
    heads_per_group = n_heads // SSM_GROUPS
    gw = d_inner // SSM_GROUPS
    gn = SSM_GROUPS * SSM_STATE
    k_w = cw_ref.shape[0]
    tail = (k_w - 1) * SUBLANES
    c = pl.program_id(1)

    @pl.when(c == 0)
    def _():
        ctail[...] = jnp.zeros(ctail.shape, F32)
        state[...] = jnp.zeros(state.shape, F32)

    u = xbc_ref[...]
    prev = ctail[...]
    sub8 = lax.broadcasted_iota(jnp.int32, (SUBLANES, u.shape[1]), 0)
    conv = cw_ref[k_w - 1:k_w, :] * u + cb_ref[...]
    for s in range(1, k_w):
        fixed = []
        for i in range(s):
            r0 = l - (s - i) * SUBLANES
            p0 = tail - (s - i) * SUBLANES
            fixed.append(jnp.where(sub8 == 0, pltpu.roll(prev[p0:p0 + SUBLANES, :], 1, 0),
                                   pltpu.roll(u[r0:r0 + SUBLANES, :], 1, 0)))
        delayed = jnp.concatenate(fixed + [u[0:l - s * SUBLANES, :]], axis=0)
        conv = conv + cw_ref[k_w - 1 - s:k_w - s, :] * delayed
    ctail[...] = u[l - tail:l, :]
    act = _silu(conv)
    xs = act[:, :d_inner]
    bm = act[:, d_inner:d_inner + gn].astype(BF16)
    cm = act[:, d_inner + gn:].astype(BF16)

    pre = dt_ref[...] + dtb_ref[...]
    dtv = jnp.maximum(pre, 0.0) + jnp.log(1.0 + jnp.exp(-jnp.abs(pre)))
    a = dtv * (-jnp.exp(alog_ref[...]))

    t_row = _chunk_time(lax.broadcasted_iota(jnp.int32, (l, l), 0))
    t_col = _chunk_time(lax.broadcasted_iota(jnp.int32, (l, l), 1))
    tri = jnp.where(t_row >= t_col, 1.0, 0.0).astype(BF16)
    a_hi, a_mid, a_lo = _split3(a)
    cs = _dot(tri, a_hi) + _dot(tri, a_mid) + _dot(tri, a_lo)
    cml = cs - jnp.log(dtv)

    cs_t = cs.T
    dt_t = dtv.T
    tot = cs_t[:, l - 1:l]
    w_t = dt_t * jnp.exp(tot - cs_t)
    g_t = jnp.broadcast_to(jnp.exp(tot), (LANES, l))
    e_t = jnp.exp(cs_t)
    xs_t = xs.T
    later = t_col >= t_row

    for g in range(SSM_GROUPS):
        b_g = bm[:, g * SSM_STATE:(g + 1) * SSM_STATE]
        c_g = cm[:, g * SSM_STATE:(g + 1) * SSM_STATE]
        cbt = _dot_nt(b_g, c_g)
        yoff = _dot_nt(state[g].astype(BF16), c_g)
        for hl in range(heads_per_group):
            h = g * heads_per_group + hl
            ch = slice(h * SSM_HEAD_DIM, (h + 1) * SSM_HEAD_DIM)
            gh = slice(hl * SSM_HEAD_DIM, (hl + 1) * SSM_HEAD_DIM)
            x_h = xs_t[ch, :]
            colv = jnp.broadcast_to(cml[:, h:h + 1], (l, l))
            rowv = jnp.broadcast_to(cs_t[h:h + 1, :], (l, l))
            mt = (cbt * jnp.exp(jnp.where(later, rowv - colv, NEG))).astype(BF16)
            y_h = _dot(x_h.astype(BF16), mt) + yoff[gh, :] * e_t[h:h + 1, :] + x_h * dskip_ref[h]
            yt_scr[ch, :] = y_h
            d_s = _dot((x_h * w_t[h:h + 1, :]).astype(BF16), b_g)
            state[g, gh, :] = g_t[h:h + 1, :] * state[g, gh, :] + d_s

    zg = _silu(z_ref[...])
    for g in range(SSM_GROUPS):
        sl = slice(g * gw, (g + 1) * gw)
        y = yt_scr[sl, :].T * zg[:, sl]
        o_ref[:, sl] = _interleave_rows(_rms(y, ng_ref[:, sl]).astype(BF16), inverse=True)


def _ssd(z, xbc, dt, cw, cb, dtb, alog, dskip, ng, batch, seq):
    t, d_inner = z.shape
    conv_dim = xbc.shape[1]
    l = SSM_CHUNK
    nc = seq // l
    row = lambda b, c: (b * nc + c, 0)
    return pl.pallas_call(
        _ssd_kernel, grid=(batch, nc),
        in_specs=[pl.BlockSpec((l, d_inner), row), pl.BlockSpec((l, conv_dim), row),
                  pl.BlockSpec((l, LANES), row),
                  _const_spec(cw.shape), _const_spec(cb.shape), _const_spec(dtb.shape),
                  _const_spec(alog.shape), pl.BlockSpec(memory_space=pltpu.SMEM), _const_spec(ng.shape)],
        out_specs=pl.BlockSpec((l, d_inner), row),
        out_shape=jax.ShapeDtypeStruct((t, d_inner), BF16),
        scratch_shapes=[pltpu.VMEM(((cw.shape[0] - 1) * SUBLANES, conv_dim), F32),
                        pltpu.VMEM((SSM_GROUPS, d_inner // SSM_GROUPS, SSM_STATE), F32),
                        pltpu.VMEM((d_inner, l), F32)],
        compiler_params=_params("arbitrary", "arbitrary"), name="ssd",
    )(z, xbc, dt, cw, cb, dtb, alog, dskip, ng)


def _ffn_kernel(tiles_per_seq, y_ref, wmix_ref, x_ref, g_ref, wup_ref, cw_ref, cb_ref, wdn_ref,
                o_ref, cbuf):
    tm = x_ref.shape[0]
    f = wdn_ref.shape[0]
    halo = SUBLANES
    i = pl.program_id(0)
    first = (i % tiles_per_seq) == 0

    @pl.when(first)
    def _():
        cbuf[0:halo, :] = jnp.zeros((halo, cbuf.shape[1]), F32)

    @pl.when(jnp.logical_not(first))
    def _():
        cbuf[0:halo, :] = cbuf[tm:tm + halo, :]

    x = x_ref[...] + _dot(y_ref[...], wmix_ref[...])
    xn = _rms(x, g_ref[...]).astype(BF16)
    u = _dot(xn, wup_ref[...])
    cbuf[halo:halo + tm, :] = u
    k_w = cw_ref.shape[0]
    conv = cw_ref[k_w - 1:k_w, :] * u + cb_ref[...]
    for s in range(1, k_w):
        conv = conv + cw_ref[k_w - 1 - s:k_w - s, :] * cbuf[halo - s:halo - s + tm, :]
    hid = (_silu(conv[:, :f]) * conv[:, f:]).astype(BF16)
    o_ref[...] = x + _dot(hid, wdn_ref[...])


def _ffn(y, wmix, x, g, wup, cw, cb, wdn, seq, name):
    t, d = x.shape
    tm = FFN_TILE
    return pl.pallas_call(
        functools.partial(_ffn_kernel, seq // tm), grid=(t // tm,),
        in_specs=[pl.BlockSpec((tm, y.shape[1]), lambda i: (i, 0)), _const_spec(wmix.shape),
                  pl.BlockSpec((tm, d), lambda i: (i, 0)), _const_spec((1, d)),
                  _const_spec(wup.shape), _const_spec(cw.shape), _const_spec((1, cb.shape[0])),
                  _const_spec(wdn.shape)],
        out_specs=pl.BlockSpec((tm, d), lambda i: (i, 0)),
        out_shape=jax.ShapeDtypeStruct((t, d), F32),
        scratch_shapes=[pltpu.VMEM((tm + SUBLANES, wup.shape[1]), F32)],
        compiler_params=_params("arbitrary"), name=name,
    )(y, wmix, x, g.reshape(1, d), wup, cw, cb.reshape(1, -1), wdn)


def _pair_norm(blk, g2):
    lane = lax.broadcasted_iota(jnp.int32, blk.shape, 1)
    lo = lane < ATT_HEAD_DIM
    sq = blk * blk
    s_lo = jnp.sum(jnp.where(lo, sq, 0.0), axis=-1, keepdims=True)
    s_hi = jnp.sum(jnp.where(lo, 0.0, sq), axis=-1, keepdims=True)
    inv = 1.0 / ATT_HEAD_DIM
    rs = jnp.where(lo, lax.rsqrt(s_lo * inv + EPS), lax.rsqrt(s_hi * inv + EPS))
    return blk * rs * g2


def _kv_kernel(x_ref, g_ref, w_ref, kg_ref, k_ref, v_ref):
    n_heads = k_ref.shape[0]
    xn = _rms(x_ref[...], g_ref[...]).astype(BF16)
    kv = _dot(xn, w_ref[...])
    kd = n_heads * LANES
    for h in range(n_heads):
        sl = slice(h * LANES, (h + 1) * LANES)
        k_ref[h] = _pair_norm(kv[:, sl], kg_ref[...]).astype(k_ref.dtype)
        v_ref[h] = kv[:, kd + h * LANES:kd + (h + 1) * LANES].astype(v_ref.dtype)


def _kv_proj(x, g, w, kg2, n_heads):
    t, d = x.shape
    tm = ROW_TILE
    hm = jax.ShapeDtypeStruct((n_heads, t, LANES), BF16)
    hspec = pl.BlockSpec((n_heads, tm, LANES), lambda i: (0, i, 0))
    return pl.pallas_call(
        _kv_kernel, grid=(t // tm,),
        in_specs=[pl.BlockSpec((tm, d), lambda i: (i, 0)), _const_spec((1, d)),
                  _const_spec(w.shape), _const_spec((1, LANES))],
        out_specs=[hspec, hspec], out_shape=[hm, hm],
        compiler_params=_params("arbitrary"), name="kv_proj",
    )(x, g.reshape(1, d), w, kg2)


def _bucket_thresholds():
    n = np.arange(1, 4 * MAX_DISTANCE, dtype=np.int64)
    max_exact = NUM_BUCKETS // 2
    nf = n.astype(np.float32)
    large = max_exact + (np.log(nf / np.float32(max_exact)) / np.float32(math.log(MAX_DISTANCE / max_exact))
                         * np.float32(NUM_BUCKETS - max_exact)).astype(np.int32)
    large = np.minimum(large, NUM_BUCKETS - 1)
    bucket = np.where(n < max_exact, n, large)
    return [int(n[np.argmax(bucket >= b)]) for b in range(max_exact + 1, NUM_BUCKETS)]


def _bias_kernel(thresholds, rb_ref, o_ref):
    t = o_ref.shape[2]
    h = pl.program_id(0)
    max_exact = NUM_BUCKETS // 2
    row = lax.broadcasted_iota(jnp.int32, (t, t), 0)
    col = lax.broadcasted_iota(jnp.int32, (t, t), 1)
    far = rb_ref[NUM_BUCKETS - 1, h]
    for kind in range(2):
        d = row - col + kind * t
        n = jnp.maximum(d, 0)
        bucket = jnp.minimum(n, max_exact)
        for thr in thresholds:
            bucket = bucket + jnp.where(n >= thr, 1, 0)
        val = jnp.zeros((t, t), F32)
        for b in range(NUM_BUCKETS):
            val = jnp.where(bucket == b, (rb_ref[b, h] - far) * LOG2E, val)
        if kind == 0:
            val = jnp.where(d >= 0, val, NEG)
        o_ref[0, kind] = val


def _bias_tiles(rel_bias, n_heads, t):
    return pl.pallas_call(
        functools.partial(_bias_kernel, _bucket_thresholds()), grid=(n_heads,),
        in_specs=[pl.BlockSpec(memory_space=pltpu.SMEM)],
        out_specs=pl.BlockSpec((1, 2, t, t), lambda h: (h, 0, 0, 0)),
        out_shape=jax.ShapeDtypeStruct((n_heads, 2, t, t), F32),
        compiler_params=_params("arbitrary"), name="rel_bias_tiles",
    )(rel_bias)


def _attn_kernel(lam_init, x_ref, g_ref, qw_ref, qg_ref, lv_ref, sg_ref, k_ref, v_ref, bias_ref,
                 o_ref, q_scr, m_scr, acc_scr):
    t = x_ref.shape[0]
    n_heads = k_ref.shape[0]
    qi = pl.program_id(1)

    xn = _rms(x_ref[...], g_ref[...]).astype(BF16)
    q = _dot(xn, qw_ref[...])
    lane = lax.broadcasted_iota(jnp.int32, (t, LANES), 1)
    lo = lane < ATT_HEAD_DIM
    scale = ATT_HEAD_DIM ** -0.5 * LOG2E
    for h in range(n_heads):
        qn = _pair_norm(q[:, h * LANES:(h + 1) * LANES], qg_ref[...]) * scale
        q_scr[h, 0:t, :] = jnp.where(lo, qn, 0.0).astype(BF16)
        q_scr[h, t:2 * t, :] = jnp.where(lo, 0.0, qn).astype(BF16)

    lv = lv_ref[...]
    lam = (jnp.exp(jnp.sum(lv[0:1] * lv[1:2], axis=-1, keepdims=True))
           - jnp.exp(jnp.sum(lv[2:3] * lv[3:4], axis=-1, keepdims=True)) + lam_init)

    ones = jnp.ones((t, LANES), BF16)

    m_scr[...] = jnp.full(m_scr.shape, NEG, F32)
    acc_scr[...] = jnp.zeros(acc_scr.shape, F32)

    def step(j, kind):
        start = pl.multiple_of(j * t, t)
        for h in range(n_heads):
            kt = k_ref[h, pl.ds(start, t), :]
            vx = jnp.concatenate([v_ref[h, pl.ds(start, t), :], ones], axis=1)
            for r0 in range(0, 2 * t, ATT_ROWS):
                rows = slice(r0, r0 + ATT_ROWS)
                s = _dot_nt(q_scr[h, rows, :], kt)
                if kind is not None:
                    b0 = r0 % t
                    s = s + bias_ref[h, kind, b0:b0 + ATT_ROWS, :]
                m_prev = m_scr[h, rows, :]
                m_new = jnp.maximum(m_prev, jnp.max(s, axis=-1, keepdims=True))
                alpha = jnp.exp2(m_prev - m_new)
                p = jnp.exp2(s - jnp.concatenate([m_new] * (t // LANES), axis=1)).astype(BF16)
                acc_scr[h, rows, :] = (jnp.concatenate([alpha, alpha], axis=1) * acc_scr[h, rows, :]
                                       + _dot(p, vx))
                m_scr[h, rows, :] = m_new

    def far_step(j, c):
        step(j, None)
        return c

    lax.fori_loop(0, jnp.maximum(qi - 1, 0), far_step, 0)

    @pl.when(qi >= 1)
    def _():
        step(qi - 1, 1)

    step(qi, 0)

    for h in range(n_heads):
        acc = acc_scr[h]
        o = acc[:, :LANES] / acc[:, LANES:]
        d = o[:t] - lam * o[t:]
        d = _rms(d, sg_ref[...]) * (1.0 - lam_init)
        o_ref[:, h * LANES:(h + 1) * LANES] = d.astype(o_ref.dtype)


def _attention(x, g, qw, qg2, lam_vecs, sg, k, v, bias, lam_init, batch, seq):
    tt, d = x.shape
    n_heads = k.shape[0]
    t = ATT_TILE
    nq = seq // t
    kd = qw.shape[1]
    kv_spec = pl.BlockSpec((n_heads, seq, LANES), lambda b, i: (0, b, 0))
    return pl.pallas_call(
        functools.partial(_attn_kernel, lam_init), grid=(batch, nq),
        in_specs=[pl.BlockSpec((t, d), lambda b, i: (b * nq + i, 0)), _const_spec((1, d)),
                  _const_spec(qw.shape), _const_spec((1, LANES)), _const_spec(lam_vecs.shape),
                  _const_spec((1, LANES)), kv_spec, kv_spec, _const_spec(bias.shape)],
        out_specs=pl.BlockSpec((t, kd), lambda b, i: (b * nq + i, 0)),
        out_shape=jax.ShapeDtypeStruct((tt, kd), BF16),
        scratch_shapes=[pltpu.VMEM((n_heads, 2 * t, LANES), BF16),
                        pltpu.VMEM((n_heads, 2 * t, LANES), F32),
                        pltpu.VMEM((n_heads, 2 * t, 2 * LANES), F32)],
        compiler_params=_params("arbitrary", "arbitrary"), name="diff_attention",
    )(x, g.reshape(1, d), qw, qg2, lam_vecs, sg.reshape(1, LANES), k, v, bias)


def kernel(x, ssm_ln_g, ssm_in_w, ssm_conv_w, ssm_conv_b, ssm_dt_bias, ssm_a_log, ssm_d, ssm_norm_g, ssm_out_w, kv_ln_g, kv_w, k_norm_g, rel_bias, attn_ln_g, q_w, q_norm_g, lam_vecs, subln_g, attn_out_w, ffn_ln_g, ffn_up_w, ffn_conv_w, ffn_conv_b, ffn_down_w):
    batch, seq, d = x.shape
    t = batch * seq
    n_a = ssm_in_w.shape[0]
    depth = ffn_up_w.shape[0]
    h = x.reshape(t, d)
    k_sh = v_sh = bias = None
    n_att_heads = q_w.shape[2] // LANES

    for layer in range(depth):
        if layer < n_a:
            i = layer
            d_inner = ssm_out_w.shape[1]
            n_ssm_heads = ssm_dt_bias.shape[1]
            conv_dim = ssm_conv_w.shape[2]
            w_in = ssm_in_w[i].astype(BF16)
            w_z = w_in[:, :d_inner]
            w_xbc = w_in[:, d_inner:d_inner + conv_dim]
            w_dt = jnp.pad(w_in[:, d_inner + conv_dim:], ((0, 0), (0, LANES - n_ssm_heads)))
            z, xbc, dt = _norm_proj(h, ssm_ln_g[i], [w_z, w_xbc, w_dt], [F32, F32, F32], "ssm_in_proj")
            pad_h = ((0, 0), (0, LANES - n_ssm_heads))
            mix = _ssd(z, xbc, dt, ssm_conv_w[i], ssm_conv_b[i].reshape(1, -1),
                       jnp.pad(ssm_dt_bias[i].reshape(1, -1), pad_h),
                       jnp.pad(ssm_a_log[i].reshape(1, -1), pad_h),
                       ssm_d[i], ssm_norm_g[i].reshape(1, -1), batch, seq)
            w_mix = ssm_out_w[i].astype(BF16)
        else:
            j = layer - n_a
            if j == 0:
                k_sh, v_sh = _kv_proj(h, kv_ln_g, kv_w.astype(BF16),
                                      jnp.tile(k_norm_g, 2).reshape(1, LANES), n_att_heads)
                bias = _bias_tiles(rel_bias, n_att_heads, ATT_TILE)
            lam_init = 0.8 - 0.6 * math.exp(-0.3 * layer)
            mix = _attention(h, attn_ln_g[j], q_w[j].astype(BF16),
                             jnp.tile(q_norm_g[j], 2).reshape(1, LANES), lam_vecs[j], subln_g[j],
                             k_sh, v_sh, bias, lam_init, batch, seq)
            w_mix = attn_out_w[j].astype(BF16)
        h = _ffn(mix, w_mix, h, ffn_ln_g[layer], ffn_up_w[layer].astype(BF16), ffn_conv_w[layer],
                 ffn_conv_b[layer], ffn_down_w[layer].astype(BF16), seq, f"conv_ffn_{layer}")
    return h.reshape(batch, seq, d)
```

```python
import functools
import math

import numpy as np
import jax
import jax.numpy as jnp
from jax import lax
from jax.experimental import pallas as pl
from jax.experimental.pallas import tpu as pltpu

F32 = jnp.float32
BF16 = jnp.bfloat16

EPS = 1e-6
NEG = -1e30
LOG2E = math.log2(math.e)

SSM_GROUPS = 4
SSM_STATE = 128
SSM_HEAD_DIM = 64
SSM_CHUNK = 128
ATT_HEAD_DIM = 64
NUM_BUCKETS = 32
MAX_DISTANCE = 128
LANES = 128
SUBLANES = 8
VMEM_LIMIT = 56 * 1024 * 1024

ROW_TILE = 512
FFN_TILE = 512
ATT_TILE = 256
ATT_ROWS = 128


def _params(*sem):
    return pltpu.CompilerParams(dimension_semantics=sem, vmem_limit_bytes=VMEM_LIMIT)


def _const_spec(shape):
    nd = len(shape)
    return pl.BlockSpec(shape, lambda *_: (0,) * nd, pipeline_mode=pl.Buffered(1))


def _rms(x, g):
    ms = jnp.mean(x * x, axis=-1, keepdims=True)
    return x * lax.rsqrt(ms + EPS) * g


def _silu(x):
    h = 0.5 * x
    return h * jnp.tanh(h) + h


def _split3(v):
    hi = v.astype(BF16)
    r1 = v - hi.astype(F32)
    mid = r1.astype(BF16)
    lo = (r1 - mid.astype(F32)).astype(BF16)
    return hi, mid, lo


def _dot(a, b):
    return jnp.dot(a, b, preferred_element_type=F32)


def _dot_nt(a, b):
    return lax.dot_general(a, b, (((1,), (1,)), ((), ())), preferred_element_type=F32)


def _in_proj_kernel(x_ref, g_ref, w_ref, wdt_ref, z_ref, xbc_ref, dt_ref):
    d_inner = z_ref.shape[1]
    conv_dim = xbc_ref.shape[1]
    xn = _interleave_rows(_rms(x_ref[...], g_ref[...]).astype(BF16))
    z_ref[...] = _dot(xn, w_ref[:, :d_inner])
    xbc_ref[...] = _dot(xn, w_ref[:, d_inner:d_inner + conv_dim])
    dt_ref[...] = _dot(xn, wdt_ref[...])


def _in_proj(x, g, w, wdt, d_inner, conv_dim):
    t, d = x.shape
    tm = ROW_TILE
    widths = (d_inner, conv_dim, LANES)
    return pl.pallas_call(
        _in_proj_kernel, grid=(t // tm,),
        in_specs=[pl.BlockSpec((tm, d), lambda i: (i, 0)), _const_spec((1, d)),
                  _const_spec(w.shape), _const_spec(wdt.shape)],
        out_specs=[pl.BlockSpec((tm, n), lambda i: (i, 0)) for n in widths],
        out_shape=[jax.ShapeDtypeStruct((t, n), F32) for n in widths],
        compiler_params=_params("arbitrary"), name="ssm_in_proj",
    )(x, g.reshape(1, d), w, wdt)


GROUPS_PER_CHUNK = SSM_CHUNK // SUBLANES


def _chunk_time(r):
    return r // SUBLANES + GROUPS_PER_CHUNK * (r % SUBLANES)


def _interleave_rows(v, inverse=False):
    l = SSM_CHUNK
    r = lax.broadcasted_iota(jnp.int32, (l, l), 0)
    c = lax.broadcasted_iota(jnp.int32, (l, l), 1)
    hit = (r == _chunk_time(c)) if inverse else (c == _chunk_time(r))
    perm = jnp.where(hit, 1.0, 0.0).astype(BF16)
    out = [_dot(perm, v[b:b + l, :]).astype(BF16) for b in range(0, v.shape[0], l)]
    return out[0] if len(out) == 1 else jnp.concatenate(out, axis=0)


def _ssd_kernel(z_ref, xbc_ref, dt_ref, cw_ref, cb_ref, dtb_ref, alog_ref, dskip_ref, ng_ref,
                o_ref, ctail, state, yt_scr):
    l = SSM_CHUNK
    d_inner = z_ref.shape[1]
    n_heads = d_inner // SSM_HEAD_DIM
    heads_per_group = n_heads // SSM_GROUPS
    gw = d_inner // SSM_GROUPS
    gn = SSM_GROUPS * SSM_STATE
    k_w = cw_ref.shape[0]
    tail = (k_w - 1) * SUBLANES
    c = pl.program_id(1)

    @pl.when(c == 0)
    def _():
        ctail[...] = jnp.zeros(ctail.shape, F32)
        state[...] = jnp.zeros(state.shape, F32)

    u = xbc_ref[...]
    prev = ctail[...]
    sub8 = lax.broadcasted_iota(jnp.int32, (SUBLANES, u.shape[1]), 0)
    conv = cw_ref[k_w - 1:k_w, :] * u + cb_ref[...]
    for s in range(1, k_w):
        fixed = []
        for i in range(s):
            r0 = l - (s - i) * SUBLANES
            p0 = tail - (s - i) * SUBLANES
            fixed.append(jnp.where(sub8 == 0, pltpu.roll(prev[p0:p0 + SUBLANES, :], 1, 0),
                                   pltpu.roll(u[r0:r0 + SUBLANES, :], 1, 0)))
        delayed = jnp.concatenate(fixed + [u[0:l - s * SUBLANES, :]], axis=0)
        conv = conv + cw_ref[k_w - 1 - s:k_w - s, :] * delayed
    ctail[...] = u[l - tail:l, :]
    act = _silu(conv)
    xs = act[:, :d_inner]
    bm = act[:, d_inner:d_inner + gn].astype(BF16)
    cm = act[:, d_inner + gn:].astype(BF16)

    pre = dt_ref[...] + dtb_ref[...]
    dtv = jnp.maximum(pre, 0.0) + jnp.log(1.0 + jnp.exp(-jnp.abs(pre)))
    a = dtv * (-jnp.exp(alog_ref[...]))

    t_row = _chunk_time(lax.broadcasted_iota(jnp.int32, (l, l), 0))
    t_col = _chunk_time(lax.broadcasted_iota(jnp.int32, (l, l), 1))
    tri = jnp.where(t_row >= t_col, 1.0, 0.0).astype(BF16)
    a_hi, a_mid, a_lo = _split3(a)
    cs = _dot(tri, a_hi) + _dot(tri, a_mid) + _dot(tri, a_lo)
    cml = cs - jnp.log(dtv)

    cs_t = cs.T
    dt_t = dtv.T
    tot = cs_t[:, l - 1:l]
    w_t = dt_t * jnp.exp(tot - cs_t)
    g_t = jnp.broadcast_to(jnp.exp(tot), (LANES, l))
    e_t = jnp.exp(cs_t)
    xs_t = xs.T
    later = t_col >= t_row

    for g in range(SSM_GROUPS):
        b_g = bm[:, g * SSM_STATE:(g + 1) * SSM_STATE]
        c_g = cm[:, g * SSM_STATE:(g + 1) * SSM_STATE]
        cbt = _dot_nt(b_g, c_g)
        yoff = _dot_nt(state[g].astype(BF16), c_g)
        for hl in range(heads_per_group):
            h = g * heads_per_group + hl
            ch = slice(h * SSM_HEAD_DIM, (h + 1) * SSM_HEAD_DIM)
            gh = slice(hl * SSM_HEAD_DIM, (hl + 1) * SSM_HEAD_DIM)
            x_h = xs_t[ch, :]
            colv = jnp.broadcast_to(cml[:, h:h + 1], (l, l))
            rowv = jnp.broadcast_to(cs_t[h:h + 1, :], (l, l))
            mt = (cbt * jnp.exp(jnp.where(later, rowv - colv, NEG))).astype(BF16)
            y_h = _dot(x_h.astype(BF16), mt) + yoff[gh, :] * e_t[h:h + 1, :] + x_h * dskip_ref[h]
            yt_scr[ch, :] = y_h
            d_s = _dot((x_h * w_t[h:h + 1, :]).astype(BF16), b_g)
            state[g, gh, :] = g_t[h:h + 1, :] * state[g, gh, :] + d_s

    zg = _silu(z_ref[...])
    for g in range(SSM_GROUPS):
        sl = slice(g * gw, (g + 1) * gw)
        y = yt_scr[sl, :].T * zg[:, sl]
        o_ref[:, sl] = _interleave_rows(_rms(y, ng_ref[:, sl]).astype(BF16), inverse=True)


def _ssd(z, xbc, dt, cw, cb, dtb, alog, dskip, ng, batch, seq):
    t, d_inner = z.shape
    conv_dim = xbc.shape[1]
    l = SSM_CHUNK
    nc = seq // l
    row = lambda b, c: (b * nc + c, 0)
    return pl.pallas_call(
        _ssd_kernel, grid=(batch, nc),
        in_specs=[pl.BlockSpec((l, d_inner), row), pl.BlockSpec((l, conv_dim), row),
                  pl.BlockSpec((l, LANES), row),
                  _const_spec(cw.shape), _const_spec(cb.shape), _const_spec(dtb.shape),
                  _const_spec(alog.shape), pl.BlockSpec(memory_space=pltpu.SMEM), _const_spec(ng.shape)],
        out_specs=pl.BlockSpec((l, d_inner), row),
        out_shape=jax.ShapeDtypeStruct((t, d_inner), BF16),
        scratch_shapes=[pltpu.VMEM(((cw.shape[0] - 1) * SUBLANES, conv_dim), F32),
                        pltpu.VMEM((SSM_GROUPS, d_inner // SSM_GROUPS, SSM_STATE), F32),
                        pltpu.VMEM((d_inner, l), F32)],
        compiler_params=_params("arbitrary", "arbitrary"), name="ssd",
    )(z, xbc, dt, cw, cb, dtb, alog, dskip, ng)


def _ffn_kernel(tiles_per_seq, y_ref, wmix_ref, x_ref, g_ref, wup_ref, cw_ref, cb_ref, wdn_ref,
                o_ref, cbuf):
    tm = x_ref.shape[0]
    f = wdn_ref.shape[0]
    halo = SUBLANES
    i = pl.program_id(0)
    first = (i % tiles_per_seq) == 0

    @pl.when(first)
    def _():
        cbuf[0:halo, :] = jnp.zeros((halo, cbuf.shape[1]), F32)

    @pl.when(jnp.logical_not(first))
    def _():
        cbuf[0:halo, :] = cbuf[tm:tm + halo, :]

    x = x_ref[...] + _dot(y_ref[...], wmix_ref[...])
    xn = _rms(x, g_ref[...]).astype(BF16)
    u = _dot(xn, wup_ref[...])
    cbuf[halo:halo + tm, :] = u
    k_w = cw_ref.shape[0]
    conv = cw_ref[k_w - 1:k_w, :] * u + cb_ref[...]
    for s in range(1, k_w):
        conv = conv + cw_ref[k_w - 1 - s:k_w - s, :] * cbuf[halo - s:halo - s + tm, :]
    hid = (_silu(conv[:, :f]) * conv[:, f:]).astype(BF16)
    o_ref[...] = x + _dot(hid, wdn_ref[...])


def _layer_spec(stack, layer):
    nd = stack.ndim - 1
    return pl.BlockSpec((None,) + stack.shape[1:], lambda *_: (layer,) + (0,) * nd,
                        pipeline_mode=pl.Buffered(1))


def _ffn(y, wmix, x, g, wup, cw, cb, wdn, layer, seq, name):
    t, d = x.shape
    tm = FFN_TILE
    return pl.pallas_call(
        functools.partial(_ffn_kernel, seq // tm), grid=(t // tm,),
        in_specs=[pl.BlockSpec((tm, y.shape[1]), lambda i: (i, 0)), _const_spec(wmix.shape),
                  pl.BlockSpec((tm, d), lambda i: (i, 0)), _const_spec((1, d)),
                  _layer_spec(wup, layer), _const_spec(cw.shape), _const_spec((1, cb.shape[0])),
                  _layer_spec(wdn, layer)],
        out_specs=pl.BlockSpec((tm, d), lambda i: (i, 0)),
        out_shape=jax.ShapeDtypeStruct((t, d), F32),
        scratch_shapes=[pltpu.VMEM((tm + SUBLANES, wup.shape[2]), F32)],
        compiler_params=_params("arbitrary"), name=name,
    )(y, wmix, x, g.reshape(1, d), wup, cw, cb.reshape(1, -1), wdn)


def _pair_norm(blk, g2):
    lane = lax.broadcasted_iota(jnp.int32, blk.shape, 1)
    lo = lane < ATT_HEAD_DIM
    sq = blk * blk
    s_lo = jnp.sum(jnp.where(lo, sq, 0.0), axis=-1, keepdims=True)
    s_hi = jnp.sum(jnp.where(lo, 0.0, sq), axis=-1, keepdims=True)
    inv = 1.0 / ATT_HEAD_DIM
    rs = jnp.where(lo, lax.rsqrt(s_lo * inv + EPS), lax.rsqrt(s_hi * inv + EPS))
    return blk * rs * g2


def _kv_kernel(x_ref, g_ref, w_ref, kg_ref, k_ref, v_ref):
    n_heads = k_ref.shape[0]
    xn = _rms(x_ref[...], g_ref[...]).astype(BF16)
    kv = _dot(xn, w_ref[...])
    kd = n_heads * LANES
    for h in range(n_heads):
        sl = slice(h * LANES, (h + 1) * LANES)
        k_ref[h] = _pair_norm(kv[:, sl], kg_ref[...]).astype(k_ref.dtype)
        v_ref[h] = kv[:, kd + h * LANES:kd + (h + 1) * LANES].astype(v_ref.dtype)


def _kv_proj(x, g, w, kg2, n_heads):
    t, d = x.shape
    tm = ROW_TILE
    hm = jax.ShapeDtypeStruct((n_heads, t, LANES), BF16)
    hspec = pl.BlockSpec((n_heads, tm, LANES), lambda i: (0, i, 0))
    return pl.pallas_call(
        _kv_kernel, grid=(t // tm,),
        in_specs=[pl.BlockSpec((tm, d), lambda i: (i, 0)), _const_spec((1, d)),
                  _const_spec(w.shape), _const_spec((1, LANES))],
        out_specs=[hspec, hspec], out_shape=[hm, hm],
        compiler_params=_params("arbitrary"), name="kv_proj",
    )(x, g.reshape(1, d), w, kg2)


def _bucket_thresholds():
    n = np.arange(1, 4 * MAX_DISTANCE, dtype=np.int64)
    max_exact = NUM_BUCKETS // 2
    nf = n.astype(np.float32)
    large = max_exact + (np.log(nf / np.float32(max_exact)) / np.float32(math.log(MAX_DISTANCE / max_exact))
                         * np.float32(NUM_BUCKETS - max_exact)).astype(np.int32)
    large = np.minimum(large, NUM_BUCKETS - 1)
    bucket = np.where(n < max_exact, n, large)
    return [int(n[np.argmax(bucket >= b)]) for b in range(max_exact + 1, NUM_BUCKETS)]


def _bias_kernel(thresholds, rb_ref, o_ref):
    t = o_ref.shape[1]
    h = pl.program_id(0)
    max_exact = NUM_BUCKETS // 2
    row = lax.broadcasted_iota(jnp.int32, (t, t), 0)
    col = lax.broadcasted_iota(jnp.int32, (t, t), 1)
    far = rb_ref[NUM_BUCKETS - 1, h]
    for tiles_left in (1, 0):
        d = row - col + tiles_left * t
        n = jnp.maximum(d, 0)
        bucket = jnp.minimum(n, max_exact)
        for thr in thresholds:
            bucket = bucket + jnp.where(n >= thr, 1, 0)
        val = jnp.zeros((t, t), F32)
        for b in range(NUM_BUCKETS):
            val = jnp.where(bucket == b, (rb_ref[b, h] - far) * LOG2E, val)
        if tiles_left == 0:
            val = jnp.where(d >= 0, val, NEG)
        o_ref[0, :, (1 - tiles_left) * t:(2 - tiles_left) * t] = val


def _bias_tiles(rel_bias, n_heads, t):
    return pl.pallas_call(
        functools.partial(_bias_kernel, _bucket_thresholds()), grid=(n_heads,),
        in_specs=[pl.BlockSpec(memory_space=pltpu.SMEM)],
        out_specs=pl.BlockSpec((1, t, 2 * t), lambda h: (h, 0, 0)),
        out_shape=jax.ShapeDtypeStruct((n_heads, t, 2 * t), F32),
        compiler_params=_params("arbitrary"), name="rel_bias_tiles",
    )(rel_bias)


def _attn_kernel(lam_init, x_ref, g_ref, qw_ref, qg_ref, lv_ref, sg_ref, k_ref, v_ref, bias_ref,
                 o_ref, q_scr, m_scr, acc_scr):
    t = x_ref.shape[0]
    n_heads = k_ref.shape[0]
    qi = pl.program_id(1)

    xn = _rms(x_ref[...], g_ref[...]).astype(BF16)
    q = _dot(xn, qw_ref[...])
    lane = lax.broadcasted_iota(jnp.int32, (t, LANES), 1)
    lo = lane < ATT_HEAD_DIM
    scale = ATT_HEAD_DIM ** -0.5 * LOG2E
    for h in range(n_heads):
        qn = _pair_norm(q[:, h * LANES:(h + 1) * LANES], qg_ref[...]) * scale
        q_scr[h, 0:t, :] = jnp.where(lo, qn, 0.0).astype(BF16)
        q_scr[h, t:2 * t, :] = jnp.where(lo, 0.0, qn).astype(BF16)

    lv = lv_ref[...]
    lam = (jnp.exp(jnp.sum(lv[0:1] * lv[1:2], axis=-1, keepdims=True))
           - jnp.exp(jnp.sum(lv[2:3] * lv[3:4], axis=-1, keepdims=True)) + lam_init)

    m_scr[...] = jnp.full(m_scr.shape, NEG, F32)
    acc_scr[...] = jnp.zeros(acc_scr.shape, F32)

    def step(j, tiles, bias_col):
        w = tiles * t
        start = pl.multiple_of(j * t, t)
        ones = jnp.ones((w, LANES), BF16)
        for h in range(n_heads):
            kt = k_ref[h, pl.ds(start, w), :]
            vx = jnp.concatenate([v_ref[h, pl.ds(start, w), :], ones], axis=1)
            for r0 in range(0, 2 * t, ATT_ROWS):
                rows = slice(r0, r0 + ATT_ROWS)
                s = _dot_nt(q_scr[h, rows, :], kt)
                if bias_col is not None:
                    b0 = r0 % t
                    s = s + bias_ref[h, b0:b0 + ATT_ROWS, bias_col:bias_col + w]
                m_prev = m_scr[h, rows, :]
                m_new = jnp.maximum(m_prev, jnp.max(s, axis=-1, keepdims=True))
                alpha = jnp.exp2(m_prev - m_new)
                p = jnp.exp2(s - jnp.concatenate([m_new] * (w // LANES), axis=1)).astype(BF16)
                acc_scr[h, rows, :] = (jnp.concatenate([alpha, alpha], axis=1) * acc_scr[h, rows, :]
                                       + _dot(p, vx))
                m_scr[h, rows, :] = m_new

    n_far = jnp.maximum(qi - 1, 0)

    def far_step(jj, c):
        step(2 * jj, 2, None)
        return c

    lax.fori_loop(0, n_far // 2, far_step, 0)

    @pl.when(n_far % 2 == 1)
    def _():
        step(n_far - 1, 1, None)

    @pl.when(qi == 0)
    def _():
        step(0, 1, t)

    @pl.when(qi >= 1)
    def _():
        step(qi - 1, 2, 0)

    for h in range(n_heads):
        acc = acc_scr[h]
        o = acc[:, :LANES] / acc[:, LANES:]
        d = o[:t] - lam * o[t:]
        d = _rms(d, sg_ref[...]) * (1.0 - lam_init)
        o_ref[:, h * LANES:(h + 1) * LANES] = d.astype(o_ref.dtype)


def _attention(x, g, qw, qg2, lam_vecs, sg, k, v, bias, lam_init, batch, seq):
    tt, d = x.shape
    n_heads = k.shape[0]
    t = ATT_TILE
    nq = seq // t
    kd = qw.shape[1]
    kv_spec = pl.BlockSpec((n_heads, seq, LANES), lambda b, i: (0, b, 0))
    return pl.pallas_call(
        functools.partial(_attn_kernel, lam_init), grid=(batch, nq),
        in_specs=[pl.BlockSpec((t, d), lambda b, i: (b * nq + i, 0)), _const_spec((1, d)),
                  _const_spec(qw.shape), _const_spec((1, LANES)), _const_spec(lam_vecs.shape),
                  _const_spec((1, LANES)), kv_spec, kv_spec, _const_spec(bias.shape)],
        out_specs=pl.BlockSpec((t, kd), lambda b, i: (b * nq + i, 0)),
        out_shape=jax.ShapeDtypeStruct((tt, kd), BF16),
        scratch_shapes=[pltpu.VMEM((n_heads, 2 * t, LANES), BF16),
                        pltpu.VMEM((n_heads, 2 * t, LANES), F32),
                        pltpu.VMEM((n_heads, 2 * t, 2 * LANES), F32)],
        compiler_params=_params("arbitrary", "arbitrary"), name="diff_attention",
    )(x, g.reshape(1, d), qw, qg2, lam_vecs, sg.reshape(1, LANES), k, v, bias)


def kernel(x, ssm_ln_g, ssm_in_w, ssm_conv_w, ssm_conv_b, ssm_dt_bias, ssm_a_log, ssm_d, ssm_norm_g, ssm_out_w, kv_ln_g, kv_w, k_norm_g, rel_bias, attn_ln_g, q_w, q_norm_g, lam_vecs, subln_g, attn_out_w, ffn_ln_g, ffn_up_w, ffn_conv_w, ffn_conv_b, ffn_down_w):
    batch, seq, d = x.shape
    t = batch * seq
    n_a = ssm_in_w.shape[0]
    depth = ffn_up_w.shape[0]
    h = x.reshape(t, d)
    k_sh = v_sh = bias = None
    n_att_heads = q_w.shape[2] // LANES
    ffn_up = ffn_up_w.astype(BF16)
    ffn_down = ffn_down_w.astype(BF16)

    for layer in range(depth):
        if layer < n_a:
            i = layer
            d_inner = ssm_out_w.shape[1]
            n_ssm_heads = ssm_dt_bias.shape[1]
            conv_dim = ssm_conv_w.shape[2]
            w_dt = jnp.pad(ssm_in_w[i][:, d_inner + conv_dim:].astype(BF16),
                           ((0, 0), (0, LANES - n_ssm_heads)))
            z, xbc, dt = _in_proj(h, ssm_ln_g[i], ssm_in_w[i].astype(BF16), w_dt, d_inner, conv_dim)
            pad_h = ((0, 0), (0, LANES - n_ssm_heads))
            mix = _ssd(z, xbc, dt, ssm_conv_w[i], ssm_conv_b[i].reshape(1, -1),
                       jnp.pad(ssm_dt_bias[i].reshape(1, -1), pad_h),
                       jnp.pad(ssm_a_log[i].reshape(1, -1), pad_h),
                       ssm_d[i], ssm_norm_g[i].reshape(1, -1), batch, seq)
            w_mix = ssm_out_w[i].astype(BF16)
        else:
            j = layer - n_a
            if j == 0:
                k_sh, v_sh = _kv_proj(h, kv_ln_g, kv_w.astype(BF16),
                                      jnp.tile(k_norm_g, 2).reshape(1, LANES), n_att_heads)
                bias = _bias_tiles(rel_bias, n_att_heads, ATT_TILE)
            lam_init = 0.8 - 0.6 * math.exp(-0.3 * layer)
            mix = _attention(h, attn_ln_g[j], q_w[j].astype(BF16),
                             jnp.tile(q_norm_g[j], 2).reshape(1, LANES), lam_vecs[j], subln_g[j],
                             k_sh, v_sh, bias, lam_init, batch, seq)
            w_mix = attn_out_w[j].astype(BF16)
        h = _ffn(mix, w_mix, h, ffn_ln_g[layer], ffn_up, ffn_conv_w[layer], ffn_conv_b[layer],
                 ffn_down, layer, seq, f"conv_ffn_{layer}")
    return h.reshape(batch, seq, d)
```

```python
import functools
import math

import numpy as np
import jax
import jax.numpy as jnp
from jax import lax
from jax.experimental import pallas as pl
from jax.experimental.pallas import tpu as pltpu

F32 = jnp.float32
BF16 = jnp.bfloat16

EPS = 1e-6
NEG = -1e30
LOG2E = math.log2(math.e)

SSM_GROUPS = 4
SSM_STATE = 128
SSM_HEAD_DIM = 64
SSM_CHUNK = 128
ATT_HEAD_DIM = 64
NUM_BUCKETS = 32
MAX_DISTANCE = 128
LANES = 128
SUBLANES = 8
VMEM_LIMIT = 56 * 1024 * 1024

ROW_TILE = 512
FFN_TILE = 512
ATT_TILE = 256
ATT_ROWS = 128


def _params(*sem):
    return pltpu.CompilerParams(dimension_semantics=sem, vmem_limit_bytes=VMEM_LIMIT)


def _const_spec(shape):
    nd = len(shape)
    return pl.BlockSpec(shape, lambda *_: (0,) * nd, pipeline_mode=pl.Buffered(1))


def _rms(x, g):
    ms = jnp.mean(x * x, axis=-1, keepdims=True)
    return x * lax.rsqrt(ms + EPS) * g


def _silu(x):
    h = 0.5 * x
    return h * jnp.tanh(h) + h


def _split3(v):
    hi = v.astype(BF16)
    r1 = v - hi.astype(F32)
    mid = r1.astype(BF16)
    lo = (r1 - mid.astype(F32)).astype(BF16)
    return hi, mid, lo


def _dot(a, b):
    return jnp.dot(a, b, preferred_element_type=F32)


def _dot_nt(a, b):
    return lax.dot_general(a, b, (((1,), (1,)), ((), ())), preferred_element_type=F32)


def _in_proj_kernel(x_ref, g_ref, w_ref, wdt_ref, z_ref, xbc_ref, dt_ref):
    d_inner = z_ref.shape[1]
    conv_dim = xbc_ref.shape[1]
    xn = _interleave_rows(_rms(x_ref[...], g_ref[...]).astype(BF16))
    z_ref[...] = _dot(xn, w_ref[:, :d_inner])
    xbc_ref[...] = _dot(xn, w_ref[:, d_inner:d_inner + conv_dim])
    dt_ref[...] = _dot(xn, wdt_ref[...])


def _in_proj(x, g, w, wdt, d_inner, conv_dim):
    t, d = x.shape
    tm = ROW_TILE
    widths = (d_inner, conv_dim, LANES)
    return pl.pallas_call(
        _in_proj_kernel, grid=(t // tm,),
        in_specs=[pl.BlockSpec((tm, d), lambda i: (i, 0)), _const_spec((1, d)),
                  _const_spec(w.shape), _const_spec(wdt.shape)],
        out_specs=[pl.BlockSpec((tm, n), lambda i: (i, 0)) for n in widths],
        out_shape=[jax.ShapeDtypeStruct((t, n), F32) for n in widths],
        compiler_params=_params("arbitrary"), name="ssm_in_proj",
    )(x, g.reshape(1, d), w, wdt)


GROUPS_PER_CHUNK = SSM_CHUNK // SUBLANES


def _chunk_time(r):
    return r // SUBLANES + GROUPS_PER_CHUNK * (r % SUBLANES)


def _interleave_rows(v, inverse=False):
    l = SSM_CHUNK
    r = lax.broadcasted_iota(jnp.int32, (l, l), 0)
    c = lax.broadcasted_iota(jnp.int32, (l, l), 1)
    hit = (r == _chunk_time(c)) if inverse else (c == _chunk_time(r))
    perm = jnp.where(hit, 1.0, 0.0).astype(BF16)
    out = [_dot(perm, v[b:b + l, :]).astype(BF16) for b in range(0, v.shape[0], l)]
    return out[0] if len(out) == 1 else jnp.concatenate(out, axis=0)


def _ssd_kernel(z_ref, xbc_ref, dt_ref, cw_ref, cb_ref, dtb_ref, alog_ref, dskip_ref, ng_ref,
                o_ref, ctail, state, yt_scr):
    l = SSM_CHUNK
    d_inner = z_ref.shape[1]
    n_heads = d_inner // SSM_HEAD_DIM
    heads_per_group = n_heads // SSM_GROUPS
    gw = d_inner // SSM_GROUPS
    gn = SSM_GROUPS * SSM_STATE
    k_w = cw_ref.shape[0]
    tail = (k_w - 1) * SUBLANES
    c = pl.program_id(1)

    @pl.when(c == 0)
    def _():
        ctail[...] = jnp.zeros(ctail.shape, F32)
        state[...] = jnp.zeros(state.shape, F32)

    u = xbc_ref[...]
    prev = ctail[...]
    sub8 = lax.broadcasted_iota(jnp.int32, (SUBLANES, u.shape[1]), 0)
    cw_half = 0.5 * cw_ref[...]
    half = cw_half[k_w - 1:k_w, :] * u + 0.5 * cb_ref[...]
    for s in range(1, k_w):
        fixed = []
        for i in range(s):
            r0 = l - (s - i) * SUBLANES
            p0 = tail - (s - i) * SUBLANES
            fixed.append(jnp.where(sub8 == 0, pltpu.roll(prev[p0:p0 + SUBLANES, :], 1, 0),
                                   pltpu.roll(u[r0:r0 + SUBLANES, :], 1, 0)))
        delayed = jnp.concatenate(fixed + [u[0:l - s * SUBLANES, :]], axis=0)
        half = half + cw_half[k_w - 1 - s:k_w - s, :] * delayed
    ctail[...] = u[l - tail:l, :]
    act = half * jnp.tanh(half) + half
    xs = act[:, :d_inner]
    bm = act[:, d_inner:d_inner + gn].astype(BF16)
    cm = act[:, d_inner + gn:].astype(BF16)

    pre = dt_ref[...] + dtb_ref[...]
    dtv = jnp.maximum(pre, 0.0) + jnp.log(1.0 + jnp.exp(-jnp.abs(pre)))
    a = dtv * (-jnp.exp(alog_ref[...]))

    t_row = _chunk_time(lax.broadcasted_iota(jnp.int32, (l, l), 0))
    t_col = _chunk_time(lax.broadcasted_iota(jnp.int32, (l, l), 1))
    tri = jnp.where(t_row >= t_col, 1.0, 0.0).astype(BF16)
    a_hi, a_mid, a_lo = _split3(a)
    cs = (_dot(tri, a_hi) + _dot(tri, a_mid) + _dot(tri, a_lo)) * LOG2E
    cml = cs - jnp.log(dtv) * LOG2E

    cs_t = cs.T
    dt_t = dtv.T
    tot = cs_t[:, l - 1:l]
    w_t = dt_t * jnp.exp2(tot - cs_t)
    g_t = jnp.broadcast_to(jnp.exp2(tot), (LANES, l))
    e_t = jnp.exp2(cs_t)
    xs_t = xs.T
    later = t_col >= t_row

    for g in range(SSM_GROUPS):
        b_g = bm[:, g * SSM_STATE:(g + 1) * SSM_STATE]
        c_g = cm[:, g * SSM_STATE:(g + 1) * SSM_STATE]
        cbt = _dot_nt(b_g, c_g)
        yoff = _dot_nt(state[g].astype(BF16), c_g)
        for hl in range(heads_per_group):
            h = g * heads_per_group + hl
            ch = slice(h * SSM_HEAD_DIM, (h + 1) * SSM_HEAD_DIM)
            gh = slice(hl * SSM_HEAD_DIM, (hl + 1) * SSM_HEAD_DIM)
            x_h = xs_t[ch, :]
            colv = jnp.broadcast_to(cml[:, h:h + 1], (l, l))
            rowv = jnp.broadcast_to(cs_t[h:h + 1, :], (l, l))
            mt = (cbt * jnp.exp2(jnp.where(later, rowv - colv, NEG))).astype(BF16)
            y_h = _dot(x_h.astype(BF16), mt) + yoff[gh, :] * e_t[h:h + 1, :] + x_h * dskip_ref[h]
            yt_scr[ch, :] = y_h
            d_s = _dot((x_h * w_t[h:h + 1, :]).astype(BF16), b_g)
            state[g, gh, :] = g_t[h:h + 1, :] * state[g, gh, :] + d_s

    zg = _silu(z_ref[...])
    for g in range(SSM_GROUPS):
        sl = slice(g * gw, (g + 1) * gw)
        y = yt_scr[sl, :].T * zg[:, sl]
        o_ref[:, sl] = _interleave_rows(_rms(y, ng_ref[:, sl]).astype(BF16), inverse=True)


def _ssd(z, xbc, dt, cw, cb, dtb, alog, dskip, ng, batch, seq):
    t, d_inner = z.shape
    conv_dim = xbc.shape[1]
    l = SSM_CHUNK
    nc = seq // l
    row = lambda b, c: (b * nc + c, 0)
    return pl.pallas_call(
        _ssd_kernel, grid=(batch, nc),
        in_specs=[pl.BlockSpec((l, d_inner), row), pl.BlockSpec((l, conv_dim), row),
                  pl.BlockSpec((l, LANES), row),
                  _const_spec(cw.shape), _const_spec(cb.shape), _const_spec(dtb.shape),
                  _const_spec(alog.shape), pl.BlockSpec(memory_space=pltpu.SMEM), _const_spec(ng.shape)],
        out_specs=pl.BlockSpec((l, d_inner), row),
        out_shape=jax.ShapeDtypeStruct((t, d_inner), BF16),
        scratch_shapes=[pltpu.VMEM(((cw.shape[0] - 1) * SUBLANES, conv_dim), F32),
                        pltpu.VMEM((SSM_GROUPS, d_inner // SSM_GROUPS, SSM_STATE), F32),
                        pltpu.VMEM((d_inner, l), F32)],
        compiler_params=_params("arbitrary", "arbitrary"), name="ssd",
    )(z, xbc, dt, cw, cb, dtb, alog, dskip, ng)


def _ffn_kernel(tiles_per_seq, y_ref, wmix_ref, x_ref, g_ref, wup_ref, cw_ref, cb_ref, wdn_ref,
                o_ref, cbuf):
    tm = x_ref.shape[0]
    f = wdn_ref.shape[0]
    halo = SUBLANES
    i = pl.program_id(0)
    first = (i % tiles_per_seq) == 0

    @pl.when(first)
    def _():
        cbuf[0:halo, :] = jnp.zeros((halo, cbuf.shape[1]), F32)

    @pl.when(jnp.logical_not(first))
    def _():
        cbuf[0:halo, :] = cbuf[tm:tm + halo, :]

    x = x_ref[...] + _dot(y_ref[...], wmix_ref[...])
    xn = _rms(x, g_ref[...]).astype(BF16)
    u = _dot(xn, wup_ref[...])
    cbuf[halo:halo + tm, :] = u
    k_w = cw_ref.shape[0]
    conv = cw_ref[k_w - 1:k_w, :] * u + cb_ref[...]
    for s in range(1, k_w):
        conv = conv + cw_ref[k_w - 1 - s:k_w - s, :] * cbuf[halo - s:halo - s + tm, :]
    hid = (_silu(conv[:, :f]) * conv[:, f:]).astype(BF16)
    o_ref[...] = x + _dot(hid, wdn_ref[...])


def _layer_spec(stack, layer):
    nd = stack.ndim - 1
    return pl.BlockSpec((None,) + stack.shape[1:], lambda *_: (layer,) + (0,) * nd,
                        pipeline_mode=pl.Buffered(1))


def _ffn(y, wmix, x, g, wup, cw, cb, wdn, layer, seq, name):
    t, d = x.shape
    tm = FFN_TILE
    return pl.pallas_call(
        functools.partial(_ffn_kernel, seq // tm), grid=(t // tm,),
        in_specs=[pl.BlockSpec((tm, y.shape[1]), lambda i: (i, 0)), _const_spec(wmix.shape),
                  pl.BlockSpec((tm, d), lambda i: (i, 0)), _const_spec((1, d)),
                  _layer_spec(wup, layer), _const_spec(cw.shape), _const_spec((1, cb.shape[0])),
                  _layer_spec(wdn, layer)],
        out_specs=pl.BlockSpec((tm, d), lambda i: (i, 0)),
        out_shape=jax.ShapeDtypeStruct((t, d), F32),
        scratch_shapes=[pltpu.VMEM((tm + SUBLANES, wup.shape[2]), F32)],
        compiler_params=_params("arbitrary"), name=name,
    )(y, wmix, x, g.reshape(1, d), wup, cw, cb.reshape(1, -1), wdn)


def _pair_norm(blk, g2):
    lane = lax.broadcasted_iota(jnp.int32, blk.shape, 1)
    lo = lane < ATT_HEAD_DIM
    sq = blk * blk
    s_lo = jnp.sum(jnp.where(lo, sq, 0.0), axis=-1, keepdims=True)
    s_hi = jnp.sum(jnp.where(lo, 0.0, sq), axis=-1, keepdims=True)
    inv = 1.0 / ATT_HEAD_DIM
    rs = jnp.where(lo, lax.rsqrt(s_lo * inv + EPS), lax.rsqrt(s_hi * inv + EPS))
    return blk * rs * g2


def _qkv_kernel(x_ref, gkv_ref, gq_ref, wkv_ref, wq_ref, kg_ref, qg_ref, k_ref, v_ref, q_ref):
    n_heads = k_ref.shape[0]
    tm = x_ref.shape[0]
    t = q_ref.shape[2] // 2
    x = x_ref[...]
    xr = x * lax.rsqrt(jnp.mean(x * x, axis=-1, keepdims=True) + EPS)
    kv = _dot((xr * gkv_ref[...]).astype(BF16), wkv_ref[...])
    q = _dot((xr * gq_ref[...]).astype(BF16), wq_ref[...])
    kd = n_heads * LANES
    lane = lax.broadcasted_iota(jnp.int32, (tm, LANES), 1)
    lo = lane < ATT_HEAD_DIM
    scale = ATT_HEAD_DIM ** -0.5 * LOG2E
    for h in range(n_heads):
        sl = slice(h * LANES, (h + 1) * LANES)
        k_ref[h] = _pair_norm(kv[:, sl], kg_ref[...]).astype(k_ref.dtype)
        v_ref[h] = kv[:, kd + h * LANES:kd + (h + 1) * LANES].astype(v_ref.dtype)
        qn = _pair_norm(q[:, sl], qg_ref[...]) * scale
        q0 = jnp.where(lo, qn, 0.0).astype(q_ref.dtype)
        q1 = jnp.where(lo, 0.0, qn).astype(q_ref.dtype)
        for qt in range(tm // t):
            q_ref[h, qt, 0:t, :] = q0[qt * t:(qt + 1) * t, :]
            q_ref[h, qt, t:2 * t, :] = q1[qt * t:(qt + 1) * t, :]


def _qkv_proj(x, gkv, gq, wkv, wq, kg2, qg2, n_heads, t):
    tt, d = x.shape
    tm = ROW_TILE
    hm = jax.ShapeDtypeStruct((n_heads, tt, LANES), BF16)
    hspec = pl.BlockSpec((n_heads, tm, LANES), lambda i: (0, i, 0))
    return pl.pallas_call(
        _qkv_kernel, grid=(tt // tm,),
        in_specs=[pl.BlockSpec((tm, d), lambda i: (i, 0)), _const_spec((1, d)), _const_spec((1, d)),
                  _const_spec(wkv.shape), _const_spec(wq.shape), _const_spec((1, LANES)),
                  _const_spec((1, LANES))],
        out_specs=[hspec, hspec, pl.BlockSpec((n_heads, tm // t, 2 * t, LANES), lambda i: (0, i, 0, 0))],
        out_shape=[hm, hm, jax.ShapeDtypeStruct((n_heads, tt // t, 2 * t, LANES), BF16)],
        compiler_params=_params("arbitrary"), name="qkv_proj",
    )(x, gkv.reshape(1, d), gq.reshape(1, d), wkv, wq, kg2, qg2)


def _bucket_thresholds():
    n = np.arange(1, 4 * MAX_DISTANCE, dtype=np.int64)
    max_exact = NUM_BUCKETS // 2
    nf = n.astype(np.float32)
    large = max_exact + (np.log(nf / np.float32(max_exact)) / np.float32(math.log(MAX_DISTANCE / max_exact))
                         * np.float32(NUM_BUCKETS - max_exact)).astype(np.int32)
    large = np.minimum(large, NUM_BUCKETS - 1)
    bucket = np.where(n < max_exact, n, large)
    return [int(n[np.argmax(bucket >= b)]) for b in range(max_exact + 1, NUM_BUCKETS)]


def _bias_kernel(thresholds, rb_ref, o_ref):
    t = o_ref.shape[1]
    h = pl.program_id(0)
    max_exact = NUM_BUCKETS // 2
    row = lax.broadcasted_iota(jnp.int32, (t, t), 0)
    col = lax.broadcasted_iota(jnp.int32, (t, t), 1)
    far = rb_ref[NUM_BUCKETS - 1, h]
    for tiles_left in (1, 0):
        d = row - col + tiles_left * t
        n = jnp.maximum(d, 0)
        bucket = jnp.minimum(n, max_exact)
        for thr in thresholds:
            bucket = bucket + jnp.where(n >= thr, 1, 0)
        val = jnp.zeros((t, t), F32)
        for b in range(NUM_BUCKETS):
            val = jnp.where(bucket == b, (rb_ref[b, h] - far) * LOG2E, val)
        if tiles_left == 0:
            val = jnp.where(d >= 0, val, NEG)
        o_ref[0, :, (1 - tiles_left) * t:(2 - tiles_left) * t] = val


def _bias_tiles(rel_bias, n_heads, t):
    return pl.pallas_call(
        functools.partial(_bias_kernel, _bucket_thresholds()), grid=(n_heads,),
        in_specs=[pl.BlockSpec(memory_space=pltpu.SMEM)],
        out_specs=pl.BlockSpec((1, t, 2 * t), lambda h: (h, 0, 0)),
        out_shape=jax.ShapeDtypeStruct((n_heads, t, 2 * t), F32),
        compiler_params=_params("arbitrary"), name="rel_bias_tiles",
    )(rel_bias)


def _attn_kernel(lam_init, q_scr, lv_ref, sg_ref, k_ref, v_ref, bias_ref, o_ref, m_scr, acc_scr):
    t = o_ref.shape[0]
    n_heads = k_ref.shape[0]
    qi = pl.program_id(1)

    lv = lv_ref[...]
    lam = (jnp.exp(jnp.sum(lv[0:1] * lv[1:2], axis=-1, keepdims=True))
           - jnp.exp(jnp.sum(lv[2:3] * lv[3:4], axis=-1, keepdims=True)) + lam_init)

    m_scr[...] = jnp.full(m_scr.shape, NEG, F32)
    acc_scr[...] = jnp.zeros(acc_scr.shape, F32)

    def step(j, tiles, bias_col):
        w = tiles * t
        start = pl.multiple_of(j * t, t)
        ones = jnp.ones((w, LANES), BF16)
        for h in range(n_heads):
            kt = k_ref[h, pl.ds(start, w), :]
            vx = jnp.concatenate([v_ref[h, pl.ds(start, w), :], ones], axis=1)
            for r0 in range(0, 2 * t, ATT_ROWS):
                rows = slice(r0, r0 + ATT_ROWS)
                s = _dot_nt(q_scr[h, rows, :], kt)
                if bias_col is not None:
                    b0 = r0 % t
                    s = s + bias_ref[h, b0:b0 + ATT_ROWS, bias_col:bias_col + w]
                m_prev = m_scr[h, rows, :]
                m_new = jnp.maximum(m_prev, jnp.max(s, axis=-1, keepdims=True))
                alpha = jnp.exp2(m_prev - m_new)
                p = jnp.exp2(s - jnp.concatenate([m_new] * (w // LANES), axis=1)).astype(BF16)
                acc_scr[h, rows, :] = (jnp.concatenate([alpha, alpha], axis=1) * acc_scr[h, rows, :]
                                       + _dot(p, vx))
                m_scr[h, rows, :] = m_new

    n_far = jnp.maximum(qi - 1, 0)

    def far_step(jj, c):
        step(2 * jj, 2, None)
        return c

    lax.fori_loop(0, n_far // 2, far_step, 0)

    @pl.when(n_far % 2 == 1)
    def _():
        step(n_far - 1, 1, None)

    @pl.when(qi == 0)
    def _():
        step(0, 1, t)

    @pl.when(qi >= 1)
    def _():
        step(qi - 1, 2, 0)

    for h in range(n_heads):
        acc = acc_scr[h]
        o = acc[:, :LANES] / acc[:, LANES:]
        d = o[:t] - lam * o[t:]
        d = _rms(d, sg_ref[...]) * (1.0 - lam_init)
        o_ref[:, h * LANES:(h + 1) * LANES] = d.astype(o_ref.dtype)


def _attention(qs, lam_vecs, sg, k, v, bias, lam_init, batch, seq):
    n_heads, tt, _ = k.shape
    t = ATT_TILE
    nq = seq // t
    kd = n_heads * LANES
    kv_spec = pl.BlockSpec((n_heads, seq, LANES), lambda b, i: (0, b, 0))
    return pl.pallas_call(
        functools.partial(_attn_kernel, lam_init), grid=(batch, nq),
        in_specs=[pl.BlockSpec((n_heads, None, 2 * t, LANES), lambda b, i: (0, b * nq + i, 0, 0)),
                  _const_spec(lam_vecs.shape), _const_spec((1, LANES)), kv_spec, kv_spec,
                  _const_spec(bias.shape)],
        out_specs=pl.BlockSpec((t, kd), lambda b, i: (b * nq + i, 0)),
        out_shape=jax.ShapeDtypeStruct((tt, kd), BF16),
        scratch_shapes=[pltpu.VMEM((n_heads, 2 * t, LANES), F32),
                        pltpu.VMEM((n_heads, 2 * t, 2 * LANES), F32)],
        compiler_params=_params("arbitrary", "arbitrary"), name="diff_attention",
    )(qs, lam_vecs, sg.reshape(1, LANES), k, v, bias)


def kernel(x, ssm_ln_g, ssm_in_w, ssm_conv_w, ssm_conv_b, ssm_dt_bias, ssm_a_log, ssm_d, ssm_norm_g, ssm_out_w, kv_ln_g, kv_w, k_norm_g, rel_bias, attn_ln_g, q_w, q_norm_g, lam_vecs, subln_g, attn_out_w, ffn_ln_g, ffn_up_w, ffn_conv_w, ffn_conv_b, ffn_down_w):
    batch, seq, d = x.shape
    t = batch * seq
    n_a = ssm_in_w.shape[0]
    depth = ffn_up_w.shape[0]
    h = x.reshape(t, d)
    k_sh = v_sh = bias = None
    n_att_heads = q_w.shape[2] // LANES
    ffn_up = ffn_up_w.astype(BF16)
    ffn_down = ffn_down_w.astype(BF16)

    for layer in range(depth):
        if layer < n_a:
            i = layer
            d_inner = ssm_out_w.shape[1]
            n_ssm_heads = ssm_dt_bias.shape[1]
            conv_dim = ssm_conv_w.shape[2]
            w_dt = jnp.pad(ssm_in_w[i][:, d_inner + conv_dim:].astype(BF16),
                           ((0, 0), (0, LANES - n_ssm_heads)))
            z, xbc, dt = _in_proj(h, ssm_ln_g[i], ssm_in_w[i].astype(BF16), w_dt, d_inner, conv_dim)
            pad_h = ((0, 0), (0, LANES - n_ssm_heads))
            mix = _ssd(z, xbc, dt, ssm_conv_w[i], ssm_conv_b[i].reshape(1, -1),
                       jnp.pad(ssm_dt_bias[i].reshape(1, -1), pad_h),
                       jnp.pad(ssm_a_log[i].reshape(1, -1), pad_h),
                       ssm_d[i], ssm_norm_g[i].reshape(1, -1), batch, seq)
            w_mix = ssm_out_w[i].astype(BF16)
        else:
            j = layer - n_a
            if j == 0:
                bias = _bias_tiles(rel_bias, n_att_heads, ATT_TILE)
            k_new, v_new, qs = _qkv_proj(h, kv_ln_g, attn_ln_g[j], kv_w.astype(BF16), q_w[j].astype(BF16),
                                         jnp.tile(k_norm_g, 2).reshape(1, LANES),
                                         jnp.tile(q_norm_g[j], 2).reshape(1, LANES), n_att_heads, ATT_TILE)
            if j == 0:
                k_sh, v_sh = k_new, v_new
            lam_init = 0.8 - 0.6 * math.exp(-0.3 * layer)
            mix = _attention(qs, lam_vecs[j], subln_g[j], k_sh, v_sh, bias, lam_init, batch, seq)
            w_mix = attn_out_w[j].astype(BF16)
        h = _ffn(mix, w_mix, h, ffn_ln_g[layer], ffn_up, ffn_conv_w[layer], ffn_conv_b[layer],
                 ffn_down, layer, seq, f"conv_ffn_{layer}")
    return h.reshape(batch, seq, d)
```

```python
import functools
import math

import numpy as np
import jax
import jax.numpy as jnp
from jax import lax
from jax.experimental import pallas as pl
from jax.experimental.pallas import tpu as pltpu

F32 = jnp.float32
BF16 = jnp.bfloat16

EPS = 1e-6
NEG = -1e30
LOG2E = math.log2(math.e)

SSM_GROUPS = 4
SSM_STATE = 128
SSM_HEAD_DIM = 64
SSM_CHUNK = 128
ATT_HEAD_DIM = 64
NUM_BUCKETS = 32
MAX_DISTANCE = 128
LANES = 128
SUBLANES = 8
VMEM_LIMIT = 56 * 1024 * 1024

ROW_TILE = 512
FFN_TILE = 512
ATT_TILE = 256
ATT_ROWS = 128


def _params(*sem):
    return pltpu.CompilerParams(dimension_semantics=sem, vmem_limit_bytes=VMEM_LIMIT)


def _const_spec(shape):
    nd = len(shape)
    return pl.BlockSpec(shape, lambda *_: (0,) * nd, pipeline_mode=pl.Buffered(1))


def _rms(x, g):
    ms = jnp.mean(x * x, axis=-1, keepdims=True)
    return x * lax.rsqrt(ms + EPS) * g


def _silu(x):
    h = 0.5 * x
    return h * jnp.tanh(h) + h


def _split3(v):
    hi = v.astype(BF16)
    r1 = v - hi.astype(F32)
    mid = r1.astype(BF16)
    lo = (r1 - mid.astype(F32)).astype(BF16)
    return hi, mid, lo


def _dot(a, b):
    return jnp.dot(a, b, preferred_element_type=F32)


def _dot_nt(a, b):
    return lax.dot_general(a, b, (((1,), (1,)), ((), ())), preferred_element_type=F32)


def _in_proj_kernel(x_ref, g_ref, w_ref, wdt_ref, z_ref, xbc_ref, dt_ref):
    d_inner = z_ref.shape[1]
    conv_dim = xbc_ref.shape[1]
    xn = _interleave_rows(_rms(x_ref[...], g_ref[...]).astype(BF16))
    z_ref[...] = _dot(xn, w_ref[:, :d_inner])
    xbc_ref[...] = _dot(xn, w_ref[:, d_inner:d_inner + conv_dim])
    dt_ref[...] = _dot(xn, wdt_ref[...])


def _in_proj(x, g, w, wdt, d_inner, conv_dim):
    t, d = x.shape
    tm = ROW_TILE
    widths = (d_inner, conv_dim, LANES)
    return pl.pallas_call(
        _in_proj_kernel, grid=(t // tm,),
        in_specs=[pl.BlockSpec((tm, d), lambda i: (i, 0)), _const_spec((1, d)),
                  _const_spec(w.shape), _const_spec(wdt.shape)],
        out_specs=[pl.BlockSpec((tm, n), lambda i: (i, 0)) for n in widths],
        out_shape=[jax.ShapeDtypeStruct((t, n), F32) for n in widths],
        compiler_params=_params("arbitrary"), name="ssm_in_proj",
    )(x, g.reshape(1, d), w, wdt)


GROUPS_PER_CHUNK = SSM_CHUNK // SUBLANES


def _chunk_time(r):
    return r // SUBLANES + GROUPS_PER_CHUNK * (r % SUBLANES)


def _interleave_rows(v, inverse=False):
    l = SSM_CHUNK
    r = lax.broadcasted_iota(jnp.int32, (l, l), 0)
    c = lax.broadcasted_iota(jnp.int32, (l, l), 1)
    hit = (r == _chunk_time(c)) if inverse else (c == _chunk_time(r))
    perm = jnp.where(hit, 1.0, 0.0).astype(BF16)
    out = [_dot(perm, v[b:b + l, :]).astype(BF16) for b in range(0, v.shape[0], l)]
    return out[0] if len(out) == 1 else jnp.concatenate(out, axis=0)


def _ssd_kernel(z_ref, xbc_ref, dt_ref, cw_ref, cb_ref, dtb_ref, alog_ref, dskip_ref, ng_ref,
                o_ref, ctail, state, yt_scr):
    l = SSM_CHUNK
    d_inner = z_ref.shape[1]
    n_heads = d_inner // SSM_HEAD_DIM
    heads_per_group = n_heads // SSM_GROUPS
    gw = d_inner // SSM_GROUPS
    gn = SSM_GROUPS * SSM_STATE
    k_w = cw_ref.shape[0]
    tail = (k_w - 1) * SUBLANES
    c = pl.program_id(1)

    @pl.when(c == 0)
    def _():
        ctail[...] = jnp.zeros(ctail.shape, F32)
        state[...] = jnp.zeros(state.shape, F32)

    u = xbc_ref[...]
    prev = ctail[...]
    sub8 = lax.broadcasted_iota(jnp.int32, (SUBLANES, u.shape[1]), 0)
    cw_half = 0.5 * cw_ref[...]
    half = cw_half[k_w - 1:k_w, :] * u + 0.5 * cb_ref[...]
    for s in range(1, k_w):
        fixed = []
        for i in range(s):
            r0 = l - (s - i) * SUBLANES
            p0 = tail - (s - i) * SUBLANES
            fixed.append(jnp.where(sub8 == 0, pltpu.roll(prev[p0:p0 + SUBLANES, :], 1, 0),
                                   pltpu.roll(u[r0:r0 + SUBLANES, :], 1, 0)))
        delayed = jnp.concatenate(fixed + [u[0:l - s * SUBLANES, :]], axis=0)
        half = half + cw_half[k_w - 1 - s:k_w - s, :] * delayed
    ctail[...] = u[l - tail:l, :]
    act = half * jnp.tanh(half) + half
    xs = act[:, :d_inner]
    bm = act[:, d_inner:d_inner + gn].astype(BF16)
    cm = act[:, d_inner + gn:].astype(BF16)

    pre = dt_ref[...] + dtb_ref[...]
    dtv = jnp.maximum(pre, 0.0) + jnp.log(1.0 + jnp.exp(-jnp.abs(pre)))
    a = dtv * (-jnp.exp(alog_ref[...]))

    t_row = _chunk_time(lax.broadcasted_iota(jnp.int32, (l, l), 0))
    t_col = _chunk_time(lax.broadcasted_iota(jnp.int32, (l, l), 1))
    tri = jnp.where(t_row >= t_col, 1.0, 0.0).astype(BF16)
    a_hi, a_mid, a_lo = _split3(a)
    cs = (_dot(tri, a_hi) + _dot(tri, a_mid) + _dot(tri, a_lo)) * LOG2E
    cml = cs - jnp.log(dtv) * LOG2E

    cs_t = cs.T
    dt_t = dtv.T
    tot = cs_t[:, l - 1:l]
    w_t = dt_t * jnp.exp2(tot - cs_t)
    g_t = jnp.broadcast_to(jnp.exp2(tot), (LANES, l))
    e_t = jnp.exp2(cs_t)
    xs_t = xs.T
    later = t_col >= t_row

    for g in range(SSM_GROUPS):
        b_g = bm[:, g * SSM_STATE:(g + 1) * SSM_STATE]
        c_g = cm[:, g * SSM_STATE:(g + 1) * SSM_STATE]
        cbt = _dot_nt(b_g, c_g)
        yoff = _dot_nt(state[g].astype(BF16), c_g)
        for hl in range(heads_per_group):
            h = g * heads_per_group + hl
            ch = slice(h * SSM_HEAD_DIM, (h + 1) * SSM_HEAD_DIM)
            gh = slice(hl * SSM_HEAD_DIM, (hl + 1) * SSM_HEAD_DIM)
            x_h = xs_t[ch, :]
            colv = jnp.broadcast_to(cml[:, h:h + 1], (l, l))
            rowv = jnp.broadcast_to(cs_t[h:h + 1, :], (l, l))
            mt = (cbt * jnp.exp2(jnp.where(later, rowv - colv, NEG))).astype(BF16)
            y_h = _dot(x_h.astype(BF16), mt) + yoff[gh, :] * e_t[h:h + 1, :] + x_h * dskip_ref[h]
            yt_scr[ch, :] = y_h
            d_s = _dot((x_h * w_t[h:h + 1, :]).astype(BF16), b_g)
            state[g, gh, :] = g_t[h:h + 1, :] * state[g, gh, :] + d_s

    zg = _silu(z_ref[...])
    for g in range(SSM_GROUPS):
        sl = slice(g * gw, (g + 1) * gw)
        y = yt_scr[sl, :].T * zg[:, sl]
        o_ref[:, sl] = _interleave_rows(_rms(y, ng_ref[:, sl]).astype(BF16), inverse=True)


def _ssd(z, xbc, dt, cw, cb, dtb, alog, dskip, ng, batch, seq):
    t, d_inner = z.shape
    conv_dim = xbc.shape[1]
    l = SSM_CHUNK
    nc = seq // l
    row = lambda b, c: (b * nc + c, 0)
    return pl.pallas_call(
        _ssd_kernel, grid=(batch, nc),
        in_specs=[pl.BlockSpec((l, d_inner), row), pl.BlockSpec((l, conv_dim), row),
                  pl.BlockSpec((l, LANES), row),
                  _const_spec(cw.shape), _const_spec(cb.shape), _const_spec(dtb.shape),
                  _const_spec(alog.shape), pl.BlockSpec(memory_space=pltpu.SMEM), _const_spec(ng.shape)],
        out_specs=pl.BlockSpec((l, d_inner), row),
        out_shape=jax.ShapeDtypeStruct((t, d_inner), BF16),
        scratch_shapes=[pltpu.VMEM(((cw.shape[0] - 1) * SUBLANES, conv_dim), F32),
                        pltpu.VMEM((SSM_GROUPS, d_inner // SSM_GROUPS, SSM_STATE), F32),
                        pltpu.VMEM((d_inner, l), F32)],
        compiler_params=_params("arbitrary", "arbitrary"), name="ssd",
    )(z, xbc, dt, cw, cb, dtb, alog, dskip, ng)


def _ffn_kernel(tiles_per_seq, y_ref, wmix_ref, x_ref, g_ref, wup_ref, cw_ref, cb_ref, wdn_ref,
                o_ref, cbuf):
    tm = x_ref.shape[0]
    f = wdn_ref.shape[0]
    halo = SUBLANES
    i = pl.program_id(0)
    first = (i % tiles_per_seq) == 0

    @pl.when(first)
    def _():
        cbuf[0:halo, :] = jnp.zeros((halo, cbuf.shape[1]), F32)

    @pl.when(jnp.logical_not(first))
    def _():
        cbuf[0:halo, :] = cbuf[tm:tm + halo, :]

    x = x_ref[...] + _dot(y_ref[...], wmix_ref[...])
    xn = _rms(x, g_ref[...]).astype(BF16)
    u = _dot(xn, wup_ref[...])
    cbuf[halo:halo + tm, :] = u
    k_w = cw_ref.shape[0]
    conv = cw_ref[k_w - 1:k_w, :] * u + cb_ref[...]
    for s in range(1, k_w):
        conv = conv + cw_ref[k_w - 1 - s:k_w - s, :] * cbuf[halo - s:halo - s + tm, :]
    hid = (_silu(conv[:, :f]) * conv[:, f:]).astype(BF16)
    o_ref[...] = x + _dot(hid, wdn_ref[...])


def _layer_spec(stack, layer):
    nd = stack.ndim - 1
    return pl.BlockSpec((None,) + stack.shape[1:], lambda *_: (layer,) + (0,) * nd,
                        pipeline_mode=pl.Buffered(1))


def _ffn(y, wmix, x, g, wup, cw, cb, wdn, layer, seq, name):
    t, d = x.shape
    tm = FFN_TILE
    return pl.pallas_call(
        functools.partial(_ffn_kernel, seq // tm), grid=(t // tm,),
        in_specs=[pl.BlockSpec((tm, y.shape[1]), lambda i: (i, 0)), _const_spec(wmix.shape),
                  pl.BlockSpec((tm, d), lambda i: (i, 0)), _const_spec((1, d)),
                  _layer_spec(wup, layer), _const_spec(cw.shape), _const_spec((1, cb.shape[0])),
                  _layer_spec(wdn, layer)],
        out_specs=pl.BlockSpec((tm, d), lambda i: (i, 0)),
        out_shape=jax.ShapeDtypeStruct((t, d), F32),
        scratch_shapes=[pltpu.VMEM((tm + SUBLANES, wup.shape[2]), F32)],
        compiler_params=_params("arbitrary"), name=name,
    )(y, wmix, x, g.reshape(1, d), wup, cw, cb.reshape(1, -1), wdn)


def _pair_norm(blk, g2):
    lane = lax.broadcasted_iota(jnp.int32, blk.shape, 1)
    lo = lane < ATT_HEAD_DIM
    sq = blk * blk
    s_lo = jnp.sum(jnp.where(lo, sq, 0.0), axis=-1, keepdims=True)
    s_hi = jnp.sum(jnp.where(lo, 0.0, sq), axis=-1, keepdims=True)
    inv = 1.0 / ATT_HEAD_DIM
    rs = jnp.where(lo, lax.rsqrt(s_lo * inv + EPS), lax.rsqrt(s_hi * inv + EPS))
    return blk * rs * g2


def _qkv_kernel(x_ref, gkv_ref, gq_ref, wkv_ref, wq_ref, kg_ref, qg_ref, k_ref, v_ref, q_ref):
    n_heads = k_ref.shape[0]
    tm = x_ref.shape[0]
    t = q_ref.shape[2] // 2
    x = x_ref[...]
    xr = x * lax.rsqrt(jnp.mean(x * x, axis=-1, keepdims=True) + EPS)
    kv = _dot((xr * gkv_ref[...]).astype(BF16), wkv_ref[...])
    q = _dot((xr * gq_ref[...]).astype(BF16), wq_ref[...])
    kd = n_heads * LANES
    lane = lax.broadcasted_iota(jnp.int32, (tm, LANES), 1)
    lo = lane < ATT_HEAD_DIM
    scale = ATT_HEAD_DIM ** -0.5 * LOG2E
    for h in range(n_heads):
        sl = slice(h * LANES, (h + 1) * LANES)
        k_ref[h] = _pair_norm(kv[:, sl], kg_ref[...]).astype(k_ref.dtype)
        v_ref[h] = kv[:, kd + h * LANES:kd + (h + 1) * LANES].astype(v_ref.dtype)
        qn = _pair_norm(q[:, sl], qg_ref[...]) * scale
        q0 = jnp.where(lo, qn, 0.0).astype(q_ref.dtype)
        q1 = jnp.where(lo, 0.0, qn).astype(q_ref.dtype)
        for qt in range(tm // t):
            q_ref[h, qt, 0:t, :] = q0[qt * t:(qt + 1) * t, :]
            q_ref[h, qt, t:2 * t, :] = q1[qt * t:(qt + 1) * t, :]


def _qkv_proj(x, gkv, gq, wkv, wq, kg2, qg2, n_heads, t):
    tt, d = x.shape
    tm = ROW_TILE
    hm = jax.ShapeDtypeStruct((n_heads, tt, LANES), BF16)
    hspec = pl.BlockSpec((n_heads, tm, LANES), lambda i: (0, i, 0))
    return pl.pallas_call(
        _qkv_kernel, grid=(tt // tm,),
        in_specs=[pl.BlockSpec((tm, d), lambda i: (i, 0)), _const_spec((1, d)), _const_spec((1, d)),
                  _const_spec(wkv.shape), _const_spec(wq.shape), _const_spec((1, LANES)),
                  _const_spec((1, LANES))],
        out_specs=[hspec, hspec, pl.BlockSpec((n_heads, tm // t, 2 * t, LANES), lambda i: (0, i, 0, 0))],
        out_shape=[hm, hm, jax.ShapeDtypeStruct((n_heads, tt // t, 2 * t, LANES), BF16)],
        compiler_params=_params("arbitrary"), name="qkv_proj",
    )(x, gkv.reshape(1, d), gq.reshape(1, d), wkv, wq, kg2, qg2)


def _bucket_thresholds():
    n = np.arange(1, 4 * MAX_DISTANCE, dtype=np.int64)
    max_exact = NUM_BUCKETS // 2
    nf = n.astype(np.float32)
    large = max_exact + (np.log(nf / np.float32(max_exact)) / np.float32(math.log(MAX_DISTANCE / max_exact))
                         * np.float32(NUM_BUCKETS - max_exact)).astype(np.int32)
    large = np.minimum(large, NUM_BUCKETS - 1)
    bucket = np.where(n < max_exact, n, large)
    return [int(n[np.argmax(bucket >= b)]) for b in range(max_exact + 1, NUM_BUCKETS)]


def _bias_kernel(thresholds, rb_ref, o_ref):
    t = o_ref.shape[1]
    h = pl.program_id(0)
    max_exact = NUM_BUCKETS // 2
    row = lax.broadcasted_iota(jnp.int32, (t, t), 0)
    col = lax.broadcasted_iota(jnp.int32, (t, t), 1)
    far = rb_ref[NUM_BUCKETS - 1, h]
    for tiles_left in (1, 0):
        d = row - col + tiles_left * t
        n = jnp.maximum(d, 0)
        bucket = jnp.minimum(n, max_exact)
        for thr in thresholds:
            bucket = bucket + jnp.where(n >= thr, 1, 0)
        val = jnp.zeros((t, t), F32)
        for b in range(NUM_BUCKETS):
            val = jnp.where(bucket == b, (rb_ref[b, h] - far) * LOG2E, val)
        if tiles_left == 0:
            val = jnp.where(d >= 0, val, NEG)
        o_ref[0, :, (1 - tiles_left) * t:(2 - tiles_left) * t] = val


def _bias_tiles(rel_bias, n_heads, t):
    return pl.pallas_call(
        functools.partial(_bias_kernel, _bucket_thresholds()), grid=(n_heads,),
        in_specs=[pl.BlockSpec(memory_space=pltpu.SMEM)],
        out_specs=pl.BlockSpec((1, t, 2 * t), lambda h: (h, 0, 0)),
        out_shape=jax.ShapeDtypeStruct((n_heads, t, 2 * t), F32),
        compiler_params=_params("arbitrary"), name="rel_bias_tiles",
    )(rel_bias)


def _attn_kernel(lam_init, q_scr, lv_ref, sg_ref, k_ref, v_ref, bias_ref, o_ref, m_scr, acc_scr):
    t = o_ref.shape[0]
    n_heads = k_ref.shape[0]
    qi = pl.program_id(1)

    lv = lv_ref[...]
    lam = (jnp.exp(jnp.sum(lv[0:1] * lv[1:2], axis=-1, keepdims=True))
           - jnp.exp(jnp.sum(lv[2:3] * lv[3:4], axis=-1, keepdims=True)) + lam_init)

    def step(j, tiles, bias_col, first=False):
        w = tiles * t
        start = pl.multiple_of(j * t, t)
        ones = jnp.ones((w, LANES), BF16)
        for h in range(n_heads):
            kt = k_ref[h, pl.ds(start, w), :]
            vx = jnp.concatenate([v_ref[h, pl.ds(start, w), :], ones], axis=1)
            for r0 in range(0, 2 * t, ATT_ROWS):
                rows = slice(r0, r0 + ATT_ROWS)
                s = _dot_nt(q_scr[h, rows, :], kt)
                if bias_col is not None:
                    b0 = r0 % t
                    s = s + bias_ref[h, b0:b0 + ATT_ROWS, bias_col:bias_col + w]
                m_cur = jnp.max(s, axis=-1, keepdims=True)
                if first:
                    m_new = jnp.broadcast_to(m_cur, (ATT_ROWS, LANES))
                else:
                    m_prev = m_scr[h, rows, :]
                    m_new = jnp.maximum(m_prev, m_cur)
                p = jnp.exp2(s - jnp.concatenate([m_new] * (w // LANES), axis=1)).astype(BF16)
                pv = _dot(p, vx)
                if not first:
                    alpha = jnp.exp2(m_prev - m_new)
                    pv = jnp.concatenate([alpha, alpha], axis=1) * acc_scr[h, rows, :] + pv
                acc_scr[h, rows, :] = pv
                m_scr[h, rows, :] = m_new

    @pl.when(qi == 0)
    def _():
        step(0, 1, t, first=True)

    @pl.when(qi >= 1)
    def _():
        step(qi - 1, 2, 0, first=True)

    n_far = jnp.maximum(qi - 1, 0)

    def far_step(jj, c):
        step(2 * jj, 2, None)
        return c

    lax.fori_loop(0, n_far // 2, far_step, 0)

    @pl.when(n_far % 2 == 1)
    def _():
        step(n_far - 1, 1, None)

    for h in range(n_heads):
        acc = acc_scr[h]
        o = acc[:, :LANES] / acc[:, LANES:]
        d = o[:t] - lam * o[t:]
        d = _rms(d, sg_ref[...]) * (1.0 - lam_init)
        o_ref[:, h * LANES:(h + 1) * LANES] = d.astype(o_ref.dtype)


def _attention(qs, lam_vecs, sg, k, v, bias, lam_init, batch, seq):
    n_heads, tt, _ = k.shape
    t = ATT_TILE
    nq = seq // t
    kd = n_heads * LANES
    kv_spec = pl.BlockSpec((n_heads, seq, LANES), lambda b, i: (0, b, 0))
    return pl.pallas_call(
        functools.partial(_attn_kernel, lam_init), grid=(batch, nq),
        in_specs=[pl.BlockSpec((n_heads, None, 2 * t, LANES), lambda b, i: (0, b * nq + i, 0, 0)),
                  _const_spec(lam_vecs.shape), _const_spec((1, LANES)), kv_spec, kv_spec,
                  _const_spec(bias.shape)],
        out_specs=pl.BlockSpec((t, kd), lambda b, i: (b * nq + i, 0)),
        out_shape=jax.ShapeDtypeStruct((tt, kd), BF16),
        scratch_shapes=[pltpu.VMEM((n_heads, 2 * t, LANES), F32),
                        pltpu.VMEM((n_heads, 2 * t, 2 * LANES), F32)],
        compiler_params=_params("arbitrary", "arbitrary"), name="diff_attention",
    )(qs, lam_vecs, sg.reshape(1, LANES), k, v, bias)


def kernel(x, ssm_ln_g, ssm_in_w, ssm_conv_w, ssm_conv_b, ssm_dt_bias, ssm_a_log, ssm_d, ssm_norm_g, ssm_out_w, kv_ln_g, kv_w, k_norm_g, rel_bias, attn_ln_g, q_w, q_norm_g, lam_vecs, subln_g, attn_out_w, ffn_ln_g, ffn_up_w, ffn_conv_w, ffn_conv_b, ffn_down_w):
    batch, seq, d = x.shape
    t = batch * seq
    n_a = ssm_in_w.shape[0]
    depth = ffn_up_w.shape[0]
    h = x.reshape(t, d)
    k_sh = v_sh = bias = None
    n_att_heads = q_w.shape[2] // LANES
    ffn_up = ffn_up_w.astype(BF16)
    ffn_down = ffn_down_w.astype(BF16)

    for layer in range(depth):
        if layer < n_a:
            i = layer
            d_inner = ssm_out_w.shape[1]
            n_ssm_heads = ssm_dt_bias.shape[1]
            conv_dim = ssm_conv_w.shape[2]
            w_dt = jnp.pad(ssm_in_w[i][:, d_inner + conv_dim:].astype(BF16),
                           ((0, 0), (0, LANES - n_ssm_heads)))
            z, xbc, dt = _in_proj(h, ssm_ln_g[i], ssm_in_w[i].astype(BF16), w_dt, d_inner, conv_dim)
            pad_h = ((0, 0), (0, LANES - n_ssm_heads))
            mix = _ssd(z, xbc, dt, ssm_conv_w[i], ssm_conv_b[i].reshape(1, -1),
                       jnp.pad(ssm_dt_bias[i].reshape(1, -1), pad_h),
                       jnp.pad(ssm_a_log[i].reshape(1, -1), pad_h),
                       ssm_d[i], ssm_norm_g[i].reshape(1, -1), batch, seq)
            w_mix = ssm_out_w[i].astype(BF16)
        else:
            j = layer - n_a
            if j == 0:
                bias = _bias_tiles(rel_bias, n_att_heads, ATT_TILE)
            k_new, v_new, qs = _qkv_proj(h, kv_ln_g, attn_ln_g[j], kv_w.astype(BF16), q_w[j].astype(BF16),
                                         jnp.tile(k_norm_g, 2).reshape(1, LANES),
                                         jnp.tile(q_norm_g[j], 2).reshape(1, LANES), n_att_heads, ATT_TILE)
            if j == 0:
                k_sh, v_sh = k_new, v_new
            lam_init = 0.8 - 0.6 * math.exp(-0.3 * layer)
            mix = _attention(qs, lam_vecs[j], subln_g[j], k_sh, v_sh, bias, lam_init, batch, seq)
            w_mix = attn_out_w[j].astype(BF16)
        h = _ffn(mix, w_mix, h, ffn_ln_g[layer], ffn_up, ffn_conv_w[layer], ffn_conv_b[layer],
                 ffn_down, layer, seq, f"conv_ffn_{layer}")
    return h.reshape(batch, seq, d)
```

```python
import functools
import math

import numpy as np
import jax
import jax.numpy as jnp
from jax import lax
from jax.experimental import pallas as pl
from jax.experimental.pallas import tpu as pltpu

F32 = jnp.float32
BF16 = jnp.bfloat16

EPS = 1e-6
NEG = -1e30
LOG2E = math.log2(math.e)

SSM_GROUPS = 4
SSM_STATE = 128
SSM_HEAD_DIM = 64
SSM_CHUNK = 128
ATT_HEAD_DIM = 64
NUM_BUCKETS = 32
MAX_DISTANCE = 128
LANES = 128
SUBLANES = 8
VMEM_LIMIT = 56 * 1024 * 1024

ROW_TILE = 512
FFN_TILE = 512
ATT_TILE = 256
ATT_ROWS = 128


def _params(*sem):
    return pltpu.CompilerParams(dimension_semantics=sem, vmem_limit_bytes=VMEM_LIMIT)


def _const_spec(shape):
    nd = len(shape)
    return pl.BlockSpec(shape, lambda *_: (0,) * nd, pipeline_mode=pl.Buffered(1))


def _rms(x, g):
    ms = jnp.mean(x * x, axis=-1, keepdims=True)
    return x * lax.rsqrt(ms + EPS) * g


def _silu(x):
    h = 0.5 * x
    return h * jnp.tanh(h) + h


def _split3(v):
    hi = v.astype(BF16)
    r1 = v - hi.astype(F32)
    mid = r1.astype(BF16)
    lo = (r1 - mid.astype(F32)).astype(BF16)
    return hi, mid, lo


def _dot(a, b):
    return jnp.dot(a, b, preferred_element_type=F32)


def _dot_nt(a, b):
    return lax.dot_general(a, b, (((1,), (1,)), ((), ())), preferred_element_type=F32)


def _in_proj_kernel(tiles_per_seq, x_ref, g_ref, w_ref, wdt_ref, cw_ref, cb_ref,
                    zg_ref, xs_ref, b_ref, c_ref, dt_ref, ctail):
    tm = x_ref.shape[0]
    l = SSM_CHUNK
    d_inner = zg_ref.shape[1]
    conv_dim = cw_ref.shape[1]
    gn = b_ref.shape[1]
    tail = ctail.shape[0]

    @pl.when(pl.program_id(0) % tiles_per_seq == 0)
    def _():
        ctail[...] = jnp.zeros(ctail.shape, F32)

    xn = _interleave_rows(_rms(x_ref[...], g_ref[...]).astype(BF16))
    xbc = _dot(xn, w_ref[:, d_inner:d_inner + conv_dim])
    zg_ref[...] = _silu(_dot(xn, w_ref[:, :d_inner]))
    dt_ref[...] = _dot(xn, wdt_ref[...])
    cw_half = 0.5 * cw_ref[...]
    cb_half = 0.5 * cb_ref[...]
    prev = ctail[...]
    for c0 in range(0, tm, l):
        u = xbc[c0:c0 + l, :]
        half = _interleaved_conv(u, prev, cw_half, cb_half)
        act = half * jnp.tanh(half) + half
        xs_ref[c0 // l] = act[:, :d_inner].T
        b_ref[c0:c0 + l, :] = act[:, d_inner:d_inner + gn].astype(b_ref.dtype)
        c_ref[c0:c0 + l, :] = act[:, d_inner + gn:].astype(c_ref.dtype)
        prev = u[l - tail:l, :]
    ctail[...] = prev


def _in_proj(x, g, w, wdt, cw, cb, d_inner, seq):
    t, d = x.shape
    tm = ROW_TILE
    l = SSM_CHUNK
    conv_dim = cw.shape[1]
    gn = (conv_dim - d_inner) // 2
    rows = lambda n, dt: (pl.BlockSpec((tm, n), lambda i: (i, 0)), jax.ShapeDtypeStruct((t, n), dt))
    outs = (rows(d_inner, F32),
            (pl.BlockSpec((tm // l, d_inner, l), lambda i: (i, 0, 0)),
             jax.ShapeDtypeStruct((t // l, d_inner, l), F32)),
            rows(gn, BF16), rows(gn, BF16), rows(LANES, F32))
    return pl.pallas_call(
        functools.partial(_in_proj_kernel, seq // tm), grid=(t // tm,),
        in_specs=[pl.BlockSpec((tm, d), lambda i: (i, 0)), _const_spec((1, d)),
                  _const_spec(w.shape), _const_spec(wdt.shape), _const_spec(cw.shape),
                  _const_spec(cb.shape)],
        out_specs=[spec for spec, _ in outs],
        out_shape=[shape for _, shape in outs],
        scratch_shapes=[pltpu.VMEM(((cw.shape[0] - 1) * SUBLANES, conv_dim), F32)],
        compiler_params=_params("arbitrary"), name="ssm_in_proj",
    )(x, g.reshape(1, d), w, wdt, cw, cb)


GROUPS_PER_CHUNK = SSM_CHUNK // SUBLANES


def _chunk_time(r):
    return r // SUBLANES + GROUPS_PER_CHUNK * (r % SUBLANES)


def _interleave_rows(v, inverse=False):
    l = SSM_CHUNK
    r = lax.broadcasted_iota(jnp.int32, (l, l), 0)
    c = lax.broadcasted_iota(jnp.int32, (l, l), 1)
    hit = (r == _chunk_time(c)) if inverse else (c == _chunk_time(r))
    perm = jnp.where(hit, 1.0, 0.0).astype(BF16)
    out = [_dot(perm, v[b:b + l, :]).astype(BF16) for b in range(0, v.shape[0], l)]
    return out[0] if len(out) == 1 else jnp.concatenate(out, axis=0)


def _interleaved_conv(u, prev, cw, cb):
    l = u.shape[0]
    k_w = cw.shape[0]
    tail = prev.shape[0]
    sub8 = lax.broadcasted_iota(jnp.int32, (SUBLANES, u.shape[1]), 0)
    out = cw[k_w - 1:k_w, :] * u + cb
    for s in range(1, k_w):
        fixed = []
        for i in range(s):
            r0 = l - (s - i) * SUBLANES
            p0 = tail - (s - i) * SUBLANES
            fixed.append(jnp.where(sub8 == 0, pltpu.roll(prev[p0:p0 + SUBLANES, :], 1, 0),
                                   pltpu.roll(u[r0:r0 + SUBLANES, :], 1, 0)))
        delayed = jnp.concatenate(fixed + [u[0:l - s * SUBLANES, :]], axis=0)
        out = out + cw[k_w - 1 - s:k_w - s, :] * delayed
    return out


def _ssd_kernel(zg_ref, xs_ref, b_ref, c_ref, dt_ref, dtb_ref, alog_ref, dskip_ref, ng_ref,
                o_ref, state, yt_scr):
    l = SSM_CHUNK
    d_inner = xs_ref.shape[0]
    n_heads = d_inner // SSM_HEAD_DIM
    heads_per_group = n_heads // SSM_GROUPS
    gw = d_inner // SSM_GROUPS

    @pl.when(pl.program_id(1) == 0)
    def _():
        state[...] = jnp.zeros(state.shape, F32)

    xs_t = xs_ref[...]
    bm = b_ref[...]
    cm = c_ref[...]

    pre = dt_ref[...] + dtb_ref[...]
    dtv = jnp.maximum(pre, 0.0) + jnp.log(1.0 + jnp.exp(-jnp.abs(pre)))
    a = dtv * (-jnp.exp(alog_ref[...]))

    t_row = _chunk_time(lax.broadcasted_iota(jnp.int32, (l, l), 0))
    t_col = _chunk_time(lax.broadcasted_iota(jnp.int32, (l, l), 1))
    tri = jnp.where(t_row >= t_col, 1.0, 0.0).astype(BF16)
    a_hi, a_mid, a_lo = _split3(a)
    cs = (_dot(tri, a_hi) + _dot(tri, a_mid) + _dot(tri, a_lo)) * LOG2E
    cml = cs - jnp.log(dtv) * LOG2E

    cs_t = cs.T
    dt_t = dtv.T
    tot = cs_t[:, l - 1:l]
    w_t = dt_t * jnp.exp2(tot - cs_t)
    g_t = jnp.broadcast_to(jnp.exp2(tot), (LANES, l))
    e_t = jnp.exp2(cs_t)
    later = t_col >= t_row

    for g in range(SSM_GROUPS):
        b_g = bm[:, g * SSM_STATE:(g + 1) * SSM_STATE]
        c_g = cm[:, g * SSM_STATE:(g + 1) * SSM_STATE]
        cbt = _dot_nt(b_g, c_g)
        yoff = _dot_nt(state[g].astype(BF16), c_g)
        for hl in range(heads_per_group):
            h = g * heads_per_group + hl
            ch = slice(h * SSM_HEAD_DIM, (h + 1) * SSM_HEAD_DIM)
            gh = slice(hl * SSM_HEAD_DIM, (hl + 1) * SSM_HEAD_DIM)
            x_h = xs_t[ch, :]
            colv = jnp.broadcast_to(cml[:, h:h + 1], (l, l))
            rowv = jnp.broadcast_to(cs_t[h:h + 1, :], (l, l))
            mt = (cbt * jnp.exp2(jnp.where(later, rowv - colv, NEG))).astype(BF16)
            y_h = _dot(x_h.astype(BF16), mt) + yoff[gh, :] * e_t[h:h + 1, :] + x_h * dskip_ref[h]
            yt_scr[ch, :] = y_h
            d_s = _dot((x_h * w_t[h:h + 1, :]).astype(BF16), b_g)
            state[g, gh, :] = g_t[h:h + 1, :] * state[g, gh, :] + d_s

    for g in range(SSM_GROUPS):
        sl = slice(g * gw, (g + 1) * gw)
        y = yt_scr[sl, :].T * zg_ref[:, sl]
        o_ref[:, sl] = _interleave_rows(_rms(y, ng_ref[:, sl]).astype(BF16), inverse=True)


def _ssd(zg, xs, bm, cm, dt, dtb, alog, dskip, ng, batch, seq):
    t, d_inner = zg.shape
    gn = bm.shape[1]
    l = SSM_CHUNK
    nc = seq // l
    row = lambda b, c: (b * nc + c, 0)
    return pl.pallas_call(
        _ssd_kernel, grid=(batch, nc),
        in_specs=[pl.BlockSpec((l, d_inner), row),
                  pl.BlockSpec((None, d_inner, l), lambda b, c: (b * nc + c, 0, 0)),
                  pl.BlockSpec((l, gn), row), pl.BlockSpec((l, gn), row), pl.BlockSpec((l, LANES), row),
                  _const_spec(dtb.shape), _const_spec(alog.shape),
                  pl.BlockSpec(memory_space=pltpu.SMEM), _const_spec(ng.shape)],
        out_specs=pl.BlockSpec((l, d_inner), row),
        out_shape=jax.ShapeDtypeStruct((t, d_inner), BF16),
        scratch_shapes=[pltpu.VMEM((SSM_GROUPS, d_inner // SSM_GROUPS, SSM_STATE), F32),
                        pltpu.VMEM((d_inner, l), F32)],
        compiler_params=_params("arbitrary", "arbitrary"), name="ssd",
    )(zg, xs, bm, cm, dt, dtb, alog, dskip, ng)


def _ffn_kernel(tiles_per_seq, y_ref, wmix_ref, x_ref, g_ref, wup_ref, cw_ref, cb_ref, wdn_ref,
                o_ref, cbuf):
    tm = x_ref.shape[0]
    f = wdn_ref.shape[0]
    halo = SUBLANES
    i = pl.program_id(0)
    first = (i % tiles_per_seq) == 0

    @pl.when(first)
    def _():
        cbuf[0:halo, :] = jnp.zeros((halo, cbuf.shape[1]), F32)

    @pl.when(jnp.logical_not(first))
    def _():
        cbuf[0:halo, :] = cbuf[tm:tm + halo, :]

    x = x_ref[...] + _dot(y_ref[...], wmix_ref[...])
    xn = _rms(x, g_ref[...]).astype(BF16)
    u = _dot(xn, wup_ref[...])
    cbuf[halo:halo + tm, :] = u
    k_w = cw_ref.shape[0]
    conv = cw_ref[k_w - 1:k_w, :] * u + cb_ref[...]
    for s in range(1, k_w):
        conv = conv + cw_ref[k_w - 1 - s:k_w - s, :] * cbuf[halo - s:halo - s + tm, :]
    hid = (_silu(conv[:, :f]) * conv[:, f:]).astype(BF16)
    o_ref[...] = x + _dot(hid, wdn_ref[...])


def _layer_spec(stack, layer):
    nd = stack.ndim - 1
    return pl.BlockSpec((None,) + stack.shape[1:], lambda *_: (layer,) + (0,) * nd,
                        pipeline_mode=pl.Buffered(1))


def _ffn(y, wmix, x, g, wup, cw, cb, wdn, layer, seq, name):
    t, d = x.shape
    tm = FFN_TILE
    return pl.pallas_call(
        functools.partial(_ffn_kernel, seq // tm), grid=(t // tm,),
        in_specs=[pl.BlockSpec((tm, y.shape[1]), lambda i: (i, 0)), _const_spec(wmix.shape),
                  pl.BlockSpec((tm, d), lambda i: (i, 0)), _const_spec((1, d)),
                  _layer_spec(wup, layer), _const_spec(cw.shape), _const_spec((1, cb.shape[0])),
                  _layer_spec(wdn, layer)],
        out_specs=pl.BlockSpec((tm, d), lambda i: (i, 0)),
        out_shape=jax.ShapeDtypeStruct((t, d), F32),
        scratch_shapes=[pltpu.VMEM((tm + SUBLANES, wup.shape[2]), F32)],
        compiler_params=_params("arbitrary"), name=name,
    )(y, wmix, x, g.reshape(1, d), wup, cw, cb.reshape(1, -1), wdn)


def _pair_norm(blk, g2):
    lane = lax.broadcasted_iota(jnp.int32, blk.shape, 1)
    lo = lane < ATT_HEAD_DIM
    sq = blk * blk
    s_lo = jnp.sum(jnp.where(lo, sq, 0.0), axis=-1, keepdims=True)
    s_hi = jnp.sum(jnp.where(lo, 0.0, sq), axis=-1, keepdims=True)
    inv = 1.0 / ATT_HEAD_DIM
    rs = jnp.where(lo, lax.rsqrt(s_lo * inv + EPS), lax.rsqrt(s_hi * inv + EPS))
    return blk * rs * g2


def _qkv_kernel(x_ref, gkv_ref, gq_ref, wkv_ref, wq_ref, kg_ref, qg_ref, k_ref, v_ref, q_ref):
    n_heads = k_ref.shape[0]
    tm = x_ref.shape[0]
    t = q_ref.shape[2] // 2
    x = x_ref[...]
    xr = x * lax.rsqrt(jnp.mean(x * x, axis=-1, keepdims=True) + EPS)
    kv = _dot((xr * gkv_ref[...]).astype(BF16), wkv_ref[...])
    q = _dot((xr * gq_ref[...]).astype(BF16), wq_ref[...])
    kd = n_heads * LANES
    lane = lax.broadcasted_iota(jnp.int32, (tm, LANES), 1)
    lo = lane < ATT_HEAD_DIM
    scale = ATT_HEAD_DIM ** -0.5 * LOG2E
    for h in range(n_heads):
        sl = slice(h * LANES, (h + 1) * LANES)
        k_ref[h] = _pair_norm(kv[:, sl], kg_ref[...]).astype(k_ref.dtype)
        v_ref[h] = kv[:, kd + h * LANES:kd + (h + 1) * LANES].astype(v_ref.dtype)
        qn = _pair_norm(q[:, sl], qg_ref[...]) * scale
        q0 = jnp.where(lo, qn, 0.0).astype(q_ref.dtype)
        q1 = jnp.where(lo, 0.0, qn).astype(q_ref.dtype)
        for qt in range(tm // t):
            q_ref[h, qt, 0:t, :] = q0[qt * t:(qt + 1) * t, :]
            q_ref[h, qt, t:2 * t, :] = q1[qt * t:(qt + 1) * t, :]


def _qkv_proj(x, gkv, gq, wkv, wq, kg2, qg2, n_heads, t):
    tt, d = x.shape
    tm = ROW_TILE
    hm = jax.ShapeDtypeStruct((n_heads, tt, LANES), BF16)
    hspec = pl.BlockSpec((n_heads, tm, LANES), lambda i: (0, i, 0))
    return pl.pallas_call(
        _qkv_kernel, grid=(tt // tm,),
        in_specs=[pl.BlockSpec((tm, d), lambda i: (i, 0)), _const_spec((1, d)), _const_spec((1, d)),
                  _const_spec(wkv.shape), _const_spec(wq.shape), _const_spec((1, LANES)),
                  _const_spec((1, LANES))],
        out_specs=[hspec, hspec, pl.BlockSpec((n_heads, tm // t, 2 * t, LANES), lambda i: (0, i, 0, 0))],
        out_shape=[hm, hm, jax.ShapeDtypeStruct((n_heads, tt // t, 2 * t, LANES), BF16)],
        compiler_params=_params("arbitrary"), name="qkv_proj",
    )(x, gkv.reshape(1, d), gq.reshape(1, d), wkv, wq, kg2, qg2)


def _bucket_thresholds():
    n = np.arange(1, 4 * MAX_DISTANCE, dtype=np.int64)
    max_exact = NUM_BUCKETS // 2
    nf = n.astype(np.float32)
    large = max_exact + (np.log(nf / np.float32(max_exact)) / np.float32(math.log(MAX_DISTANCE / max_exact))
                         * np.float32(NUM_BUCKETS - max_exact)).astype(np.int32)
    large = np.minimum(large, NUM_BUCKETS - 1)
    bucket = np.where(n < max_exact, n, large)
    return [int(n[np.argmax(bucket >= b)]) for b in range(max_exact + 1, NUM_BUCKETS)]


def _bias_kernel(thresholds, rb_ref, o_ref):
    t = o_ref.shape[1]
    h = pl.program_id(0)
    max_exact = NUM_BUCKETS // 2
    row = lax.broadcasted_iota(jnp.int32, (t, t), 0)
    col = lax.broadcasted_iota(jnp.int32, (t, t), 1)
    far = rb_ref[NUM_BUCKETS - 1, h]
    for tiles_left in (1, 0):
        d = row - col + tiles_left * t
        n = jnp.maximum(d, 0)
        bucket = jnp.minimum(n, max_exact)
        for thr in thresholds:
            bucket = bucket + jnp.where(n >= thr, 1, 0)
        val = jnp.zeros((t, t), F32)
        for b in range(NUM_BUCKETS):
            val = jnp.where(bucket == b, (rb_ref[b, h] - far) * LOG2E, val)
        if tiles_left == 0:
            val = jnp.where(d >= 0, val, NEG)
        o_ref[0, :, (1 - tiles_left) * t:(2 - tiles_left) * t] = val


def _bias_tiles(rel_bias, n_heads, t):
    return pl.pallas_call(
        functools.partial(_bias_kernel, _bucket_thresholds()), grid=(n_heads,),
        in_specs=[pl.BlockSpec(memory_space=pltpu.SMEM)],
        out_specs=pl.BlockSpec((1, t, 2 * t), lambda h: (h, 0, 0)),
        out_shape=jax.ShapeDtypeStruct((n_heads, t, 2 * t), F32),
        compiler_params=_params("arbitrary"), name="rel_bias_tiles",
    )(rel_bias)


def _attn_kernel(lam_init, q_scr, lv_ref, sg_ref, k_ref, v_ref, bias_ref, o_ref, m_scr, acc_scr):
    t = o_ref.shape[0]
    n_heads = k_ref.shape[0]
    qi = pl.program_id(1)

    lv = lv_ref[...]
    lam = (jnp.exp(jnp.sum(lv[0:1] * lv[1:2], axis=-1, keepdims=True))
           - jnp.exp(jnp.sum(lv[2:3] * lv[3:4], axis=-1, keepdims=True)) + lam_init)

    def step(j, tiles, bias_col, first=False):
        w = tiles * t
        start = pl.multiple_of(j * t, t)
        ones = jnp.ones((w, LANES), BF16)
        for h in range(n_heads):
            kt = k_ref[h, pl.ds(start, w), :]
            vx = jnp.concatenate([v_ref[h, pl.ds(start, w), :], ones], axis=1)
            for r0 in range(0, 2 * t, ATT_ROWS):
                rows = slice(r0, r0 + ATT_ROWS)
                s = _dot_nt(q_scr[h, rows, :], kt)
                if bias_col is not None:
                    b0 = r0 % t
                    s = s + bias_ref[h, b0:b0 + ATT_ROWS, bias_col:bias_col + w]
                m_cur = jnp.max(s, axis=-1, keepdims=True)
                if first:
                    m_new = jnp.broadcast_to(m_cur, (ATT_ROWS, LANES))
                else:
                    m_prev = m_scr[h, rows, :]
                    m_new = jnp.maximum(m_prev, m_cur)
                p = jnp.exp2(s - jnp.concatenate([m_new] * (w // LANES), axis=1)).astype(BF16)
                pv = _dot(p, vx)
                if not first:
                    alpha = jnp.exp2(m_prev - m_new)
                    pv = jnp.concatenate([alpha, alpha], axis=1) * acc_scr[h, rows, :] + pv
                acc_scr[h, rows, :] = pv
                m_scr[h, rows, :] = m_new

    @pl.when(qi == 0)
    def _():
        step(0, 1, t, first=True)

    @pl.when(qi >= 1)
    def _():
        step(qi - 1, 2, 0, first=True)

    n_far = jnp.maximum(qi - 1, 0)

    def far_step(jj, c):
        step(2 * jj, 2, None)
        return c

    lax.fori_loop(0, n_far // 2, far_step, 0)

    @pl.when(n_far % 2 == 1)
    def _():
        step(n_far - 1, 1, None)

    for h in range(n_heads):
        acc = acc_scr[h]
        o = acc[:, :LANES] / acc[:, LANES:]
        d = o[:t] - lam * o[t:]
        d = _rms(d, sg_ref[...]) * (1.0 - lam_init)
        o_ref[:, h * LANES:(h + 1) * LANES] = d.astype(o_ref.dtype)


def _attention(qs, lam_vecs, sg, k, v, bias, lam_init, batch, seq):
    n_heads, tt, _ = k.shape
    t = ATT_TILE
    nq = seq // t
    kd = n_heads * LANES
    kv_spec = pl.BlockSpec((n_heads, seq, LANES), lambda b, i: (0, b, 0))
    return pl.pallas_call(
        functools.partial(_attn_kernel, lam_init), grid=(batch, nq),
        in_specs=[pl.BlockSpec((n_heads, None, 2 * t, LANES), lambda b, i: (0, b * nq + i, 0, 0)),
                  _const_spec(lam_vecs.shape), _const_spec((1, LANES)), kv_spec, kv_spec,
                  _const_spec(bias.shape)],
        out_specs=pl.BlockSpec((t, kd), lambda b, i: (b * nq + i, 0)),
        out_shape=jax.ShapeDtypeStruct((tt, kd), BF16),
        scratch_shapes=[pltpu.VMEM((n_heads, 2 * t, LANES), F32),
                        pltpu.VMEM((n_heads, 2 * t, 2 * LANES), F32)],
        compiler_params=_params("arbitrary", "arbitrary"), name="diff_attention",
    )(qs, lam_vecs, sg.reshape(1, LANES), k, v, bias)


def kernel(x, ssm_ln_g, ssm_in_w, ssm_conv_w, ssm_conv_b, ssm_dt_bias, ssm_a_log, ssm_d, ssm_norm_g, ssm_out_w, kv_ln_g, kv_w, k_norm_g, rel_bias, attn_ln_g, q_w, q_norm_g, lam_vecs, subln_g, attn_out_w, ffn_ln_g, ffn_up_w, ffn_conv_w, ffn_conv_b, ffn_down_w):
    batch, seq, d = x.shape
    t = batch * seq
    n_a = ssm_in_w.shape[0]
    depth = ffn_up_w.shape[0]
    h = x.reshape(t, d)
    k_sh = v_sh = bias = None
    n_att_heads = q_w.shape[2] // LANES
    ffn_up = ffn_up_w.astype(BF16)
    ffn_down = ffn_down_w.astype(BF16)

    for layer in range(depth):
        if layer < n_a:
            i = layer
            d_inner = ssm_out_w.shape[1]
            n_ssm_heads = ssm_dt_bias.shape[1]
            conv_dim = ssm_conv_w.shape[2]
            w_dt = jnp.pad(ssm_in_w[i][:, d_inner + conv_dim:].astype(BF16),
                           ((0, 0), (0, LANES - n_ssm_heads)))
            zg, xs, bm, cm, dt = _in_proj(h, ssm_ln_g[i], ssm_in_w[i].astype(BF16), w_dt, ssm_conv_w[i],
                                          ssm_conv_b[i].reshape(1, -1), d_inner, seq)
            pad_h = ((0, 0), (0, LANES - n_ssm_heads))
            mix = _ssd(zg, xs, bm, cm, dt, jnp.pad(ssm_dt_bias[i].reshape(1, -1), pad_h),
                       jnp.pad(ssm_a_log[i].reshape(1, -1), pad_h),
                       ssm_d[i], ssm_norm_g[i].reshape(1, -1), batch, seq)
            w_mix = ssm_out_w[i].astype(BF16)
        else:
            j = layer - n_a
            if j == 0:
                bias = _bias_tiles(rel_bias, n_att_heads, ATT_TILE)
            k_new, v_new, qs = _qkv_proj(h, kv_ln_g, attn_ln_g[j], kv_w.astype(BF16), q_w[j].astype(BF16),
                                         jnp.tile(k_norm_g, 2).reshape(1, LANES),
                                         jnp.tile(q_norm_g[j], 2).reshape(1, LANES), n_att_heads, ATT_TILE)
            if j == 0:
                k_sh, v_sh = k_new, v_new
            lam_init = 0.8 - 0.6 * math.exp(-0.3 * layer)
            mix = _attention(qs, lam_vecs[j], subln_g[j], k_sh, v_sh, bias, lam_init, batch, seq)
            w_mix = attn_out_w[j].astype(BF16)
        h = _ffn(mix, w_mix, h, ffn_ln_g[layer], ffn_up, ffn_conv_w[layer], ffn_conv_b[layer],
                 ffn_down, layer, seq, f"conv_ffn_{layer}")
    return h.reshape(batch, seq, d)
```

```python
import functools
import math

import numpy as np
import jax
import jax.numpy as jnp
from jax import lax
from jax.experimental import pallas as pl
from jax.experimental.pallas import tpu as pltpu

F32 = jnp.float32
BF16 = jnp.bfloat16

EPS = 1e-6
NEG = -1e30
LOG2E = math.log2(math.e)

SSM_GROUPS = 4
SSM_STATE = 128
SSM_HEAD_DIM = 64
SSM_CHUNK = 128
ATT_HEAD_DIM = 64
NUM_BUCKETS = 32
MAX_DISTANCE = 128
LANES = 128
SUBLANES = 8
VMEM_LIMIT = 56 * 1024 * 1024

ROW_TILE = 512
QKV_TILE = 1024
FFN_TILE = 512
ATT_TILE = 256
ATT_ROWS = 128


def _params(*sem):
    return pltpu.CompilerParams(dimension_semantics=sem, vmem_limit_bytes=VMEM_LIMIT)


def _const_spec(shape):
    nd = len(shape)
    return pl.BlockSpec(shape, lambda *_: (0,) * nd, pipeline_mode=pl.Buffered(1))


def _rms(x, g):
    ms = jnp.mean(x * x, axis=-1, keepdims=True)
    return x * lax.rsqrt(ms + EPS) * g


def _silu(x):
    h = 0.5 * x
    return h * jnp.tanh(h) + h


def _split3(v):
    hi = v.astype(BF16)
    r1 = v - hi.astype(F32)
    mid = r1.astype(BF16)
    lo = (r1 - mid.astype(F32)).astype(BF16)
    return hi, mid, lo


def _dot(a, b):
    return jnp.dot(a, b, preferred_element_type=F32)


def _dot_nt(a, b):
    return lax.dot_general(a, b, (((1,), (1,)), ((), ())), preferred_element_type=F32)


def _in_proj_kernel(tiles_per_seq, x_ref, g_ref, w_ref, wdt_ref, cw_ref, cb_ref,
                    zg_ref, xs_ref, b_ref, c_ref, dt_ref, ctail):
    tm = x_ref.shape[0]
    l = SSM_CHUNK
    d_inner = zg_ref.shape[1]
    conv_dim = cw_ref.shape[1]
    gn = b_ref.shape[1]
    tail = ctail.shape[0]

    @pl.when(pl.program_id(0) % tiles_per_seq == 0)
    def _():
        ctail[...] = jnp.zeros(ctail.shape, F32)

    xn = _interleave_rows(_rms(x_ref[...], g_ref[...]).astype(BF16))
    xbc = _dot(xn, w_ref[:, d_inner:d_inner + conv_dim])
    zg_ref[...] = _silu(_dot(xn, w_ref[:, :d_inner]))
    dt_ref[...] = _dot(xn, wdt_ref[...])
    cw_half = 0.5 * cw_ref[...]
    cb_half = 0.5 * cb_ref[...]
    prev = ctail[...]
    for c0 in range(0, tm, l):
        u = xbc[c0:c0 + l, :]
        half = _interleaved_conv(u, prev, cw_half, cb_half)
        act = half * jnp.tanh(half) + half
        xs_ref[c0 // l] = act[:, :d_inner].T
        b_ref[c0:c0 + l, :] = act[:, d_inner:d_inner + gn].astype(b_ref.dtype)
        c_ref[c0:c0 + l, :] = act[:, d_inner + gn:].astype(c_ref.dtype)
        prev = u[l - tail:l, :]
    ctail[...] = prev


def _in_proj(x, g, w, wdt, cw, cb, d_inner, seq):
    t, d = x.shape
    tm = ROW_TILE
    l = SSM_CHUNK
    conv_dim = cw.shape[1]
    gn = (conv_dim - d_inner) // 2
    rows = lambda n, dt: (pl.BlockSpec((tm, n), lambda i: (i, 0)), jax.ShapeDtypeStruct((t, n), dt))
    outs = (rows(d_inner, F32),
            (pl.BlockSpec((tm // l, d_inner, l), lambda i: (i, 0, 0)),
             jax.ShapeDtypeStruct((t // l, d_inner, l), F32)),
            rows(gn, BF16), rows(gn, BF16), rows(LANES, F32))
    return pl.pallas_call(
        functools.partial(_in_proj_kernel, seq // tm), grid=(t // tm,),
        in_specs=[pl.BlockSpec((tm, d), lambda i: (i, 0)), _const_spec((1, d)),
                  _const_spec(w.shape), _const_spec(wdt.shape), _const_spec(cw.shape),
                  _const_spec(cb.shape)],
        out_specs=[spec for spec, _ in outs],
        out_shape=[shape for _, shape in outs],
        scratch_shapes=[pltpu.VMEM(((cw.shape[0] - 1) * SUBLANES, conv_dim), F32)],
        compiler_params=_params("arbitrary"), name="ssm_in_proj",
    )(x, g.reshape(1, d), w, wdt, cw, cb)


GROUPS_PER_CHUNK = SSM_CHUNK // SUBLANES
SSD_CHUNKS_PER_STEP = 4


def _chunk_time(r):
    return r // SUBLANES + GROUPS_PER_CHUNK * (r % SUBLANES)


def _interleave_rows(v, inverse=False):
    l = SSM_CHUNK
    r = lax.broadcasted_iota(jnp.int32, (l, l), 0)
    c = lax.broadcasted_iota(jnp.int32, (l, l), 1)
    hit = (r == _chunk_time(c)) if inverse else (c == _chunk_time(r))
    perm = jnp.where(hit, 1.0, 0.0).astype(BF16)
    out = [_dot(perm, v[b:b + l, :]).astype(BF16) for b in range(0, v.shape[0], l)]
    return out[0] if len(out) == 1 else jnp.concatenate(out, axis=0)


def _interleaved_conv(u, prev, cw, cb):
    l = u.shape[0]
    k_w = cw.shape[0]
    tail = prev.shape[0]
    sub8 = lax.broadcasted_iota(jnp.int32, (SUBLANES, u.shape[1]), 0)
    out = cw[k_w - 1:k_w, :] * u + cb
    for s in range(1, k_w):
        fixed = []
        for i in range(s):
            r0 = l - (s - i) * SUBLANES
            p0 = tail - (s - i) * SUBLANES
            fixed.append(jnp.where(sub8 == 0, pltpu.roll(prev[p0:p0 + SUBLANES, :], 1, 0),
                                   pltpu.roll(u[r0:r0 + SUBLANES, :], 1, 0)))
        delayed = jnp.concatenate(fixed + [u[0:l - s * SUBLANES, :]], axis=0)
        out = out + cw[k_w - 1 - s:k_w - s, :] * delayed
    return out


def _ssd_kernel(zg_ref, xs_ref, b_ref, c_ref, dt_ref, dtb_ref, alog_ref, dskip_ref, ng_ref,
                o_ref, state, yt_scr):
    l = SSM_CHUNK
    d_inner = xs_ref.shape[1]
    n_heads = d_inner // SSM_HEAD_DIM
    heads_per_group = n_heads // SSM_GROUPS
    gw = d_inner // SSM_GROUPS

    @pl.when(pl.program_id(1) == 0)
    def _():
        state[...] = jnp.zeros(state.shape, F32)

    for ci in range(xs_ref.shape[0]):
        rows = slice(ci * l, (ci + 1) * l)
        xs_t = xs_ref[ci]
        bm = b_ref[rows, :]
        cm = c_ref[rows, :]

        pre = dt_ref[rows, :] + dtb_ref[...]
        dtv = jnp.maximum(pre, 0.0) + jnp.log(1.0 + jnp.exp(-jnp.abs(pre)))
        a = dtv * (-jnp.exp(alog_ref[...]))

        t_row = _chunk_time(lax.broadcasted_iota(jnp.int32, (l, l), 0))
        t_col = _chunk_time(lax.broadcasted_iota(jnp.int32, (l, l), 1))
        tri = jnp.where(t_row >= t_col, 1.0, 0.0).astype(BF16)
        a_hi, a_mid, a_lo = _split3(a)
        cs = (_dot(tri, a_hi) + _dot(tri, a_mid) + _dot(tri, a_lo)) * LOG2E
        cml = cs - jnp.log(dtv) * LOG2E

        cs_t = cs.T
        dt_t = dtv.T
        tot = cs_t[:, l - 1:l]
        w_t = dt_t * jnp.exp2(tot - cs_t)
        g_t = jnp.broadcast_to(jnp.exp2(tot), (LANES, l))
        e_t = jnp.exp2(cs_t)
        later = t_col >= t_row

        for g in range(SSM_GROUPS):
            b_g = bm[:, g * SSM_STATE:(g + 1) * SSM_STATE]
            c_g = cm[:, g * SSM_STATE:(g + 1) * SSM_STATE]
            cbt = _dot_nt(b_g, c_g)
            yoff = _dot_nt(state[g].astype(BF16), c_g)
            for hl in range(heads_per_group):
                h = g * heads_per_group + hl
                ch = slice(h * SSM_HEAD_DIM, (h + 1) * SSM_HEAD_DIM)
                gh = slice(hl * SSM_HEAD_DIM, (hl + 1) * SSM_HEAD_DIM)
                x_h = xs_t[ch, :]
                colv = jnp.broadcast_to(cml[:, h:h + 1], (l, l))
                rowv = jnp.broadcast_to(cs_t[h:h + 1, :], (l, l))
                mt = (cbt * jnp.exp2(jnp.where(later, rowv - colv, NEG))).astype(BF16)
                y_h = _dot(x_h.astype(BF16), mt) + yoff[gh, :] * e_t[h:h + 1, :] + x_h * dskip_ref[h]
                yt_scr[ci, ch, :] = y_h
                d_s = _dot((x_h * w_t[h:h + 1, :]).astype(BF16), b_g)
                state[g, gh, :] = g_t[h:h + 1, :] * state[g, gh, :] + d_s

        for g in range(SSM_GROUPS):
            sl = slice(g * gw, (g + 1) * gw)
            y = yt_scr[ci, sl, :].T * zg_ref[rows, sl]
            o_ref[rows, sl] = _interleave_rows(_rms(y, ng_ref[:, sl]).astype(BF16), inverse=True)


def _ssd(zg, xs, bm, cm, dt, dtb, alog, dskip, ng, batch, seq):
    t, d_inner = zg.shape
    gn = bm.shape[1]
    n = SSD_CHUNKS_PER_STEP
    l = SSM_CHUNK
    ns = seq // (n * l)
    row = lambda b, c: (b * ns + c, 0)
    return pl.pallas_call(
        _ssd_kernel, grid=(batch, ns),
        in_specs=[pl.BlockSpec((n * l, d_inner), row),
                  pl.BlockSpec((n, d_inner, l), lambda b, c: (b * ns + c, 0, 0)),
                  pl.BlockSpec((n * l, gn), row), pl.BlockSpec((n * l, gn), row),
                  pl.BlockSpec((n * l, LANES), row),
                  _const_spec(dtb.shape), _const_spec(alog.shape),
                  pl.BlockSpec(memory_space=pltpu.SMEM), _const_spec(ng.shape)],
        out_specs=pl.BlockSpec((n * l, d_inner), row),
        out_shape=jax.ShapeDtypeStruct((t, d_inner), BF16),
        scratch_shapes=[pltpu.VMEM((SSM_GROUPS, d_inner // SSM_GROUPS, SSM_STATE), F32),
                        pltpu.VMEM((n, d_inner, l), F32)],
        compiler_params=_params("arbitrary", "arbitrary"), name="ssd",
    )(zg, xs, bm, cm, dt, dtb, alog, dskip, ng)


def _ffn_kernel(tiles_per_seq, y_ref, wmix_ref, x_ref, g_ref, wup_ref, cw_ref, cb_ref, wdn_ref,
                o_ref, cbuf):
    tm = x_ref.shape[0]
    f = wdn_ref.shape[0]
    halo = SUBLANES
    i = pl.program_id(0)
    first = (i % tiles_per_seq) == 0

    @pl.when(first)
    def _():
        cbuf[0:halo, :] = jnp.zeros((halo, cbuf.shape[1]), F32)

    @pl.when(jnp.logical_not(first))
    def _():
        cbuf[0:halo, :] = cbuf[tm:tm + halo, :]

    x = x_ref[...] + _dot(y_ref[...], wmix_ref[...])
    xn = _rms(x, g_ref[...]).astype(BF16)
    u = _dot(xn, wup_ref[...])
    cbuf[halo:halo + tm, :] = u
    k_w = cw_ref.shape[0]
    conv = cw_ref[k_w - 1:k_w, :] * u + cb_ref[...]
    for s in range(1, k_w):
        conv = conv + cw_ref[k_w - 1 - s:k_w - s, :] * cbuf[halo - s:halo - s + tm, :]
    hid = (_silu(conv[:, :f]) * conv[:, f:]).astype(BF16)
    o_ref[...] = x + _dot(hid, wdn_ref[...])


def _layer_spec(stack, layer):
    nd = stack.ndim - 1
    return pl.BlockSpec((None,) + stack.shape[1:], lambda *_: (layer,) + (0,) * nd,
                        pipeline_mode=pl.Buffered(1))


def _ffn(y, wmix, x, g, wup, cw, cb, wdn, layer, seq, name):
    t, d = x.shape
    tm = FFN_TILE
    return pl.pallas_call(
        functools.partial(_ffn_kernel, seq // tm), grid=(t // tm,),
        in_specs=[pl.BlockSpec((tm, y.shape[1]), lambda i: (i, 0)), _const_spec(wmix.shape),
                  pl.BlockSpec((tm, d), lambda i: (i, 0)), _const_spec((1, d)),
                  _layer_spec(wup, layer), _const_spec(cw.shape), _const_spec((1, cb.shape[0])),
                  _layer_spec(wdn, layer)],
        out_specs=pl.BlockSpec((tm, d), lambda i: (i, 0)),
        out_shape=jax.ShapeDtypeStruct((t, d), F32),
        scratch_shapes=[pltpu.VMEM((tm + SUBLANES, wup.shape[2]), F32)],
        compiler_params=_params("arbitrary"), name=name,
    )(y, wmix, x, g.reshape(1, d), wup, cw, cb.reshape(1, -1), wdn)


def _pair_norm(blk, g2):
    lane = lax.broadcasted_iota(jnp.int32, blk.shape, 1)
    lo = lane < ATT_HEAD_DIM
    sq = blk * blk
    s_lo = jnp.sum(jnp.where(lo, sq, 0.0), axis=-1, keepdims=True)
    s_hi = jnp.sum(jnp.where(lo, 0.0, sq), axis=-1, keepdims=True)
    inv = 1.0 / ATT_HEAD_DIM
    rs = jnp.where(lo, lax.rsqrt(s_lo * inv + EPS), lax.rsqrt(s_hi * inv + EPS))
    return blk * rs * g2


def _qkv_kernel(x_ref, gkv_ref, gq_ref, wkv_ref, wq_ref, kg_ref, qg_ref, k_ref, v_ref, q_ref):
    n_heads = k_ref.shape[0]
    tm = x_ref.shape[0]
    t = q_ref.shape[2] // 2
    x = x_ref[...]
    xr = x * lax.rsqrt(jnp.mean(x * x, axis=-1, keepdims=True) + EPS)
    kv = _dot((xr * gkv_ref[...]).astype(BF16), wkv_ref[...])
    q = _dot((xr * gq_ref[...]).astype(BF16), wq_ref[...])
    kd = n_heads * LANES
    lane = lax.broadcasted_iota(jnp.int32, (tm, LANES), 1)
    lo = lane < ATT_HEAD_DIM
    scale = ATT_HEAD_DIM ** -0.5 * LOG2E
    for h in range(n_heads):
        sl = slice(h * LANES, (h + 1) * LANES)
        k_ref[h] = _pair_norm(kv[:, sl], kg_ref[...]).astype(k_ref.dtype)
        v_ref[h] = kv[:, kd + h * LANES:kd + (h + 1) * LANES].astype(v_ref.dtype)
        qn = _pair_norm(q[:, sl], qg_ref[...]) * scale
        q0 = jnp.where(lo, qn, 0.0).astype(q_ref.dtype)
        q1 = jnp.where(lo, 0.0, qn).astype(q_ref.dtype)
        for qt in range(tm // t):
            q_ref[h, qt, 0:t, :] = q0[qt * t:(qt + 1) * t, :]
            q_ref[h, qt, t:2 * t, :] = q1[qt * t:(qt + 1) * t, :]


def _qkv_proj(x, gkv, gq, wkv, wq, kg2, qg2, n_heads, t):
    tt, d = x.shape
    tm = QKV_TILE
    hm = jax.ShapeDtypeStruct((n_heads, tt, LANES), BF16)
    hspec = pl.BlockSpec((n_heads, tm, LANES), lambda i: (0, i, 0))
    return pl.pallas_call(
        _qkv_kernel, grid=(tt // tm,),
        in_specs=[pl.BlockSpec((tm, d), lambda i: (i, 0)), _const_spec((1, d)), _const_spec((1, d)),
                  _const_spec(wkv.shape), _const_spec(wq.shape), _const_spec((1, LANES)),
                  _const_spec((1, LANES))],
        out_specs=[hspec, hspec, pl.BlockSpec((n_heads, tm // t, 2 * t, LANES), lambda i: (0, i, 0, 0))],
        out_shape=[hm, hm, jax.ShapeDtypeStruct((n_heads, tt // t, 2 * t, LANES), BF16)],
        compiler_params=_params("arbitrary"), name="qkv_proj",
    )(x, gkv.reshape(1, d), gq.reshape(1, d), wkv, wq, kg2, qg2)


def _bucket_thresholds():
    n = np.arange(1, 4 * MAX_DISTANCE, dtype=np.int64)
    max_exact = NUM_BUCKETS // 2
    nf = n.astype(np.float32)
    large = max_exact + (np.log(nf / np.float32(max_exact)) / np.float32(math.log(MAX_DISTANCE / max_exact))
                         * np.float32(NUM_BUCKETS - max_exact)).astype(np.int32)
    large = np.minimum(large, NUM_BUCKETS - 1)
    bucket = np.where(n < max_exact, n, large)
    return [int(n[np.argmax(bucket >= b)]) for b in range(max_exact + 1, NUM_BUCKETS)]


def _bias_kernel(thresholds, rb_ref, o_ref):
    t = o_ref.shape[1]
    h = pl.program_id(0)
    max_exact = NUM_BUCKETS // 2
    row = lax.broadcasted_iota(jnp.int32, (t, t), 0)
    col = lax.broadcasted_iota(jnp.int32, (t, t), 1)
    far = rb_ref[NUM_BUCKETS - 1, h]
    for tiles_left in (1, 0):
        d = row - col + tiles_left * t
        n = jnp.maximum(d, 0)
        bucket = jnp.minimum(n, max_exact)
        for thr in thresholds:
            bucket = bucket + jnp.where(n >= thr, 1, 0)
        val = jnp.zeros((t, t), F32)
        for b in range(NUM_BUCKETS):
            val = jnp.where(bucket == b, (rb_ref[b, h] - far) * LOG2E, val)
        if tiles_left == 0:
            val = jnp.where(d >= 0, val, NEG)
        o_ref[0, :, (1 - tiles_left) * t:(2 - tiles_left) * t] = val


def _bias_tiles(rel_bias, n_heads, t):
    return pl.pallas_call(
        functools.partial(_bias_kernel, _bucket_thresholds()), grid=(n_heads,),
        in_specs=[pl.BlockSpec(memory_space=pltpu.SMEM)],
        out_specs=pl.BlockSpec((1, t, 2 * t), lambda h: (h, 0, 0)),
        out_shape=jax.ShapeDtypeStruct((n_heads, t, 2 * t), F32),
        compiler_params=_params("arbitrary"), name="rel_bias_tiles",
    )(rel_bias)


def _attn_kernel(lam_init, q_scr, lv_ref, sg_ref, k_ref, v_ref, bias_ref, o_ref, m_scr, acc_scr):
    t = o_ref.shape[0]
    n_heads = k_ref.shape[0]
    qi = pl.program_id(1)

    lv = lv_ref[...]
    lam = (jnp.exp(jnp.sum(lv[0:1] * lv[1:2], axis=-1, keepdims=True))
           - jnp.exp(jnp.sum(lv[2:3] * lv[3:4], axis=-1, keepdims=True)) + lam_init)

    def step(j, tiles, bias_col, first=False):
        w = tiles * t
        start = pl.multiple_of(j * t, t)
        ones = jnp.ones((w, LANES), BF16)
        for h in range(n_heads):
            kt = k_ref[h, pl.ds(start, w), :]
            vx = jnp.concatenate([v_ref[h, pl.ds(start, w), :], ones], axis=1)
            for r0 in range(0, 2 * t, ATT_ROWS):
                rows = slice(r0, r0 + ATT_ROWS)
                s = _dot_nt(q_scr[h, rows, :], kt)
                if bias_col is not None:
                    b0 = r0 % t
                    s = s + bias_ref[h, b0:b0 + ATT_ROWS, bias_col:bias_col + w]
                m_cur = jnp.max(s, axis=-1, keepdims=True)
                if first:
                    m_new = jnp.broadcast_to(m_cur, (ATT_ROWS, LANES))
                else:
                    m_prev = m_scr[h, rows, :]
                    m_new = jnp.maximum(m_prev, m_cur)
                p = jnp.exp2(s - jnp.concatenate([m_new] * (w // LANES), axis=1)).astype(BF16)
                pv = _dot(p, vx)
                if not first:
                    alpha = jnp.exp2(m_prev - m_new)
                    pv = jnp.concatenate([alpha, alpha], axis=1) * acc_scr[h, rows, :] + pv
                acc_scr[h, rows, :] = pv
                m_scr[h, rows, :] = m_new

    @pl.when(qi == 0)
    def _():
        step(0, 1, t, first=True)

    @pl.when(qi >= 1)
    def _():
        step(qi - 1, 2, 0, first=True)

    n_far = jnp.maximum(qi - 1, 0)

    def far_step(jj, c):
        step(2 * jj, 2, None)
        return c

    lax.fori_loop(0, n_far // 2, far_step, 0)

    @pl.when(n_far % 2 == 1)
    def _():
        step(n_far - 1, 1, None)

    for h in range(n_heads):
        acc = acc_scr[h]
        o = acc[:, :LANES] / acc[:, LANES:]
        d = o[:t] - lam * o[t:]
        d = _rms(d, sg_ref[...]) * (1.0 - lam_init)
        o_ref[:, h * LANES:(h + 1) * LANES] = d.astype(o_ref.dtype)


def _attention(qs, lam_vecs, sg, k, v, bias, lam_init, batch, seq):
    n_heads, tt, _ = k.shape
    t = ATT_TILE
    nq = seq // t
    kd = n_heads * LANES
    kv_spec = pl.BlockSpec((n_heads, seq, LANES), lambda b, i: (0, b, 0))
    return pl.pallas_call(
        functools.partial(_attn_kernel, lam_init), grid=(batch, nq),
        in_specs=[pl.BlockSpec((n_heads, None, 2 * t, LANES), lambda b, i: (0, b * nq + i, 0, 0)),
                  _const_spec(lam_vecs.shape), _const_spec((1, LANES)), kv_spec, kv_spec,
                  _const_spec(bias.shape)],
        out_specs=pl.BlockSpec((t, kd), lambda b, i: (b * nq + i, 0)),
        out_shape=jax.ShapeDtypeStruct((tt, kd), BF16),
        scratch_shapes=[pltpu.VMEM((n_heads, 2 * t, LANES), F32),
                        pltpu.VMEM((n_heads, 2 * t, 2 * LANES), F32)],
        compiler_params=_params("arbitrary", "arbitrary"), name="diff_attention",
    )(qs, lam_vecs, sg.reshape(1, LANES), k, v, bias)


def kernel(x, ssm_ln_g, ssm_in_w, ssm_conv_w, ssm_conv_b, ssm_dt_bias, ssm_a_log, ssm_d, ssm_norm_g, ssm_out_w, kv_ln_g, kv_w, k_norm_g, rel_bias, attn_ln_g, q_w, q_norm_g, lam_vecs, subln_g, attn_out_w, ffn_ln_g, ffn_up_w, ffn_conv_w, ffn_conv_b, ffn_down_w):
    batch, seq, d = x.shape
    t = batch * seq
    n_a = ssm_in_w.shape[0]
    depth = ffn_up_w.shape[0]
    h = x.reshape(t, d)
    k_sh = v_sh = bias = None
    n_att_heads = q_w.shape[2] // LANES
    ffn_up = ffn_up_w.astype(BF16)
    ffn_down = ffn_down_w.astype(BF16)

    for layer in range(depth):
        if layer < n_a:
            i = layer
            d_inner = ssm_out_w.shape[1]
            n_ssm_heads = ssm_dt_bias.shape[1]
            conv_dim = ssm_conv_w.shape[2]
            w_dt = jnp.pad(ssm_in_w[i][:, d_inner + conv_dim:].astype(BF16),
                           ((0, 0), (0, LANES - n_ssm_heads)))
            zg, xs, bm, cm, dt = _in_proj(h, ssm_ln_g[i], ssm_in_w[i].astype(BF16), w_dt, ssm_conv_w[i],
                                          ssm_conv_b[i].reshape(1, -1), d_inner, seq)
            pad_h = ((0, 0), (0, LANES - n_ssm_heads))
            mix = _ssd(zg, xs, bm, cm, dt, jnp.pad(ssm_dt_bias[i].reshape(1, -1), pad_h),
                       jnp.pad(ssm_a_log[i].reshape(1, -1), pad_h),
                       ssm_d[i], ssm_norm_g[i].reshape(1, -1), batch, seq)
            w_mix = ssm_out_w[i].astype(BF16)
        else:
            j = layer - n_a
            if j == 0:
                bias = _bias_tiles(rel_bias, n_att_heads, ATT_TILE)
            k_new, v_new, qs = _qkv_proj(h, kv_ln_g, attn_ln_g[j], kv_w.astype(BF16), q_w[j].astype(BF16),
                                         jnp.tile(k_norm_g, 2).reshape(1, LANES),
                                         jnp.tile(q_norm_g[j], 2).reshape(1, LANES), n_att_heads, ATT_TILE)
            if j == 0:
                k_sh, v_sh = k_new, v_new
            lam_init = 0.8 - 0.6 * math.exp(-0.3 * layer)
            mix = _attention(qs, lam_vecs[j], subln_g[j], k_sh, v_sh, bias, lam_init, batch, seq)
            w_mix = attn_out_w[j].astype(BF16)
        h = _ffn(mix, w_mix, h, ffn_ln_g[layer], ffn_up, ffn_conv_w[layer], ffn_conv_b[layer],
                 ffn_down, layer, seq, f"conv_ffn_{layer}")
    return h.reshape(batch, seq, d)
```

```python
import functools
import math

import numpy as np
import jax
import jax.numpy as jnp
from jax import lax
from jax.experimental import pallas as pl
from jax.experimental.pallas import tpu as pltpu

F32 = jnp.float32
BF16 = jnp.bfloat16

EPS = 1e-6
NEG = -1e30
LOG2E = math.log2(math.e)

SSM_GROUPS = 4
SSM_STATE = 128
SSM_HEAD_DIM = 64
SSM_CHUNK = 128
ATT_HEAD_DIM = 64
NUM_BUCKETS = 32
MAX_DISTANCE = 128
LANES = 128
SUBLANES = 8
VMEM_LIMIT = 56 * 1024 * 1024

ROW_TILE = 512
QKV_TILE = 1024
FFN_TILE = 512
ATT_TILE = 256
ATT_ROWS = 128


def _params(*sem):
    return pltpu.CompilerParams(dimension_semantics=sem, vmem_limit_bytes=VMEM_LIMIT)


def _const_spec(shape):
    nd = len(shape)
    return pl.BlockSpec(shape, lambda *_: (0,) * nd, pipeline_mode=pl.Buffered(1))


def _rms(x, g):
    ms = jnp.mean(x * x, axis=-1, keepdims=True)
    return x * lax.rsqrt(ms + EPS) * g


def _silu(x):
    h = 0.5 * x
    return h * jnp.tanh(h) + h


def _split3(v):
    hi = v.astype(BF16)
    r1 = v - hi.astype(F32)
    mid = r1.astype(BF16)
    lo = (r1 - mid.astype(F32)).astype(BF16)
    return hi, mid, lo


def _dot(a, b):
    return jnp.dot(a, b, preferred_element_type=F32)


def _dot_nt(a, b):
    return lax.dot_general(a, b, (((1,), (1,)), ((), ())), preferred_element_type=F32)


def _in_proj_kernel(tiles_per_seq, x_ref, g_ref, w_ref, wdt_ref, cw_ref, cb_ref,
                    zg_ref, xs_ref, b_ref, c_ref, dt_ref, ctail):
    tm = x_ref.shape[0]
    l = SSM_CHUNK
    d_inner = xs_ref.shape[1]
    conv_dim = cw_ref.shape[1]
    gn = b_ref.shape[1]
    tail = ctail.shape[0]

    @pl.when(pl.program_id(0) % tiles_per_seq == 0)
    def _():
        ctail[...] = jnp.zeros(ctail.shape, F32)

    xn = _interleave_rows(_rms(x_ref[...], g_ref[...]).astype(BF16))
    xbc = _dot(xn, w_ref[:, d_inner:d_inner + conv_dim])
    zg = _silu(_dot(xn, w_ref[:, :d_inner]))
    for c0 in range(0, tm, l):
        zg_ref[c0 // l] = zg[c0:c0 + l, :].T
    dt_ref[...] = _dot(xn, wdt_ref[...])
    cw_half = 0.5 * cw_ref[...]
    cb_half = 0.5 * cb_ref[...]
    prev = ctail[...]
    for c0 in range(0, tm, l):
        u = xbc[c0:c0 + l, :]
        half = _interleaved_conv(u, prev, cw_half, cb_half)
        act = half * jnp.tanh(half) + half
        xs_ref[c0 // l] = act[:, :d_inner].T
        b_ref[c0:c0 + l, :] = act[:, d_inner:d_inner + gn].astype(b_ref.dtype)
        c_ref[c0:c0 + l, :] = act[:, d_inner + gn:].astype(c_ref.dtype)
        prev = u[l - tail:l, :]
    ctail[...] = prev


def _in_proj(x, g, w, wdt, cw, cb, d_inner, seq):
    t, d = x.shape
    tm = ROW_TILE
    l = SSM_CHUNK
    conv_dim = cw.shape[1]
    gn = (conv_dim - d_inner) // 2
    rows = lambda n, dt: (pl.BlockSpec((tm, n), lambda i: (i, 0)), jax.ShapeDtypeStruct((t, n), dt))
    chunked = (pl.BlockSpec((tm // l, d_inner, l), lambda i: (i, 0, 0)),
               jax.ShapeDtypeStruct((t // l, d_inner, l), F32))
    outs = (chunked, chunked, rows(gn, BF16), rows(gn, BF16), rows(LANES, F32))
    return pl.pallas_call(
        functools.partial(_in_proj_kernel, seq // tm), grid=(t // tm,),
        in_specs=[pl.BlockSpec((tm, d), lambda i: (i, 0)), _const_spec((1, d)),
                  _const_spec(w.shape), _const_spec(wdt.shape), _const_spec(cw.shape),
                  _const_spec(cb.shape)],
        out_specs=[spec for spec, _ in outs],
        out_shape=[shape for _, shape in outs],
        scratch_shapes=[pltpu.VMEM(((cw.shape[0] - 1) * SUBLANES, conv_dim), F32)],
        compiler_params=_params("arbitrary"), name="ssm_in_proj",
    )(x, g.reshape(1, d), w, wdt, cw, cb)


GROUPS_PER_CHUNK = SSM_CHUNK // SUBLANES
SSD_CHUNKS_PER_STEP = 4


def _chunk_time(r):
    return r // SUBLANES + GROUPS_PER_CHUNK * (r % SUBLANES)


def _interleave_rows(v, inverse=False):
    l = SSM_CHUNK
    r = lax.broadcasted_iota(jnp.int32, (l, l), 0)
    c = lax.broadcasted_iota(jnp.int32, (l, l), 1)
    hit = (r == _chunk_time(c)) if inverse else (c == _chunk_time(r))
    perm = jnp.where(hit, 1.0, 0.0).astype(BF16)
    out = [_dot(perm, v[b:b + l, :]).astype(BF16) for b in range(0, v.shape[0], l)]
    return out[0] if len(out) == 1 else jnp.concatenate(out, axis=0)


def _interleaved_conv(u, prev, cw, cb):
    l = u.shape[0]
    k_w = cw.shape[0]
    tail = prev.shape[0]
    sub8 = lax.broadcasted_iota(jnp.int32, (SUBLANES, u.shape[1]), 0)
    out = cw[k_w - 1:k_w, :] * u + cb
    for s in range(1, k_w):
        fixed = []
        for i in range(s):
            r0 = l - (s - i) * SUBLANES
            p0 = tail - (s - i) * SUBLANES
            fixed.append(jnp.where(sub8 == 0, pltpu.roll(prev[p0:p0 + SUBLANES, :], 1, 0),
                                   pltpu.roll(u[r0:r0 + SUBLANES, :], 1, 0)))
        delayed = jnp.concatenate(fixed + [u[0:l - s * SUBLANES, :]], axis=0)
        out = out + cw[k_w - 1 - s:k_w - s, :] * delayed
    return out


def _ssd_kernel(zg_ref, xs_ref, b_ref, c_ref, dt_ref, dtb_ref, alog_ref, dskip_ref, ng_ref,
                o_ref, state, yt_scr):
    l = SSM_CHUNK
    d_inner = xs_ref.shape[1]
    n_heads = d_inner // SSM_HEAD_DIM
    heads_per_group = n_heads // SSM_GROUPS
    gw = d_inner // SSM_GROUPS

    @pl.when(pl.program_id(1) == 0)
    def _():
        state[...] = jnp.zeros(state.shape, F32)

    for ci in range(xs_ref.shape[0]):
        rows = slice(ci * l, (ci + 1) * l)
        xs_t = xs_ref[ci]
        bm = b_ref[rows, :]
        cm = c_ref[rows, :]

        pre = dt_ref[rows, :] + dtb_ref[...]
        dtv = jnp.maximum(pre, 0.0) + jnp.log(1.0 + jnp.exp(-jnp.abs(pre)))
        a = dtv * (-jnp.exp(alog_ref[...]))

        t_row = _chunk_time(lax.broadcasted_iota(jnp.int32, (l, l), 0))
        t_col = _chunk_time(lax.broadcasted_iota(jnp.int32, (l, l), 1))
        tri = jnp.where(t_row >= t_col, 1.0, 0.0).astype(BF16)
        a_hi, a_mid, a_lo = _split3(a)
        cs = (_dot(tri, a_hi) + _dot(tri, a_mid) + _dot(tri, a_lo)) * LOG2E
        cml = cs - jnp.log(dtv) * LOG2E

        cs_t = cs.T
        dt_t = dtv.T
        tot = cs_t[:, l - 1:l]
        w_t = dt_t * jnp.exp2(tot - cs_t)
        g_t = jnp.broadcast_to(jnp.exp2(tot), (LANES, l))
        e_t = jnp.exp2(cs_t)
        later = t_col >= t_row

        for g in range(SSM_GROUPS):
            b_g = bm[:, g * SSM_STATE:(g + 1) * SSM_STATE]
            c_g = cm[:, g * SSM_STATE:(g + 1) * SSM_STATE]
            cbt = _dot_nt(b_g, c_g)
            yoff = _dot_nt(state[g].astype(BF16), c_g)
            for hl in range(heads_per_group):
                h = g * heads_per_group + hl
                ch = slice(h * SSM_HEAD_DIM, (h + 1) * SSM_HEAD_DIM)
                gh = slice(hl * SSM_HEAD_DIM, (hl + 1) * SSM_HEAD_DIM)
                x_h = xs_t[ch, :]
                colv = jnp.broadcast_to(cml[:, h:h + 1], (l, l))
                rowv = jnp.broadcast_to(cs_t[h:h + 1, :], (l, l))
                mt = (cbt * jnp.exp2(jnp.where(later, rowv - colv, NEG))).astype(BF16)
                y_h = _dot(x_h.astype(BF16), mt) + yoff[gh, :] * e_t[h:h + 1, :] + x_h * dskip_ref[h]
                yt_scr[ci, ch, :] = y_h
                d_s = _dot((x_h * w_t[h:h + 1, :]).astype(BF16), b_g)
                state[g, gh, :] = g_t[h:h + 1, :] * state[g, gh, :] + d_s

        r_i = lax.broadcasted_iota(jnp.int32, (l, l), 0)
        c_i = lax.broadcasted_iota(jnp.int32, (l, l), 1)
        restore = jnp.where(r_i == _chunk_time(c_i), 1.0, 0.0).astype(BF16)
        for g in range(SSM_GROUPS):
            sl = slice(g * gw, (g + 1) * gw)
            yt = yt_scr[ci, sl, :] * zg_ref[ci, sl, :]
            ms = jnp.mean(yt * yt, axis=0, keepdims=True)
            yn = (yt * lax.rsqrt(ms + EPS) * ng_ref[sl, :]).astype(BF16)
            o_ref[rows, sl] = _dot_nt(restore, yn).astype(o_ref.dtype)


def _ssd(zg, xs, bm, cm, dt, dtb, alog, dskip, ng, batch, seq):
    d_inner = xs.shape[1]
    t, gn = bm.shape
    n = SSD_CHUNKS_PER_STEP
    l = SSM_CHUNK
    ns = seq // (n * l)
    row = lambda b, c: (b * ns + c, 0)
    return pl.pallas_call(
        _ssd_kernel, grid=(batch, ns),
        in_specs=[pl.BlockSpec((n, d_inner, l), lambda b, c: (b * ns + c, 0, 0)),
                  pl.BlockSpec((n, d_inner, l), lambda b, c: (b * ns + c, 0, 0)),
                  pl.BlockSpec((n * l, gn), row), pl.BlockSpec((n * l, gn), row),
                  pl.BlockSpec((n * l, LANES), row),
                  _const_spec(dtb.shape), _const_spec(alog.shape),
                  pl.BlockSpec(memory_space=pltpu.SMEM), _const_spec(ng.shape)],
        out_specs=pl.BlockSpec((n * l, d_inner), row),
        out_shape=jax.ShapeDtypeStruct((t, d_inner), BF16),
        scratch_shapes=[pltpu.VMEM((SSM_GROUPS, d_inner // SSM_GROUPS, SSM_STATE), F32),
                        pltpu.VMEM((n, d_inner, l), F32)],
        compiler_params=_params("arbitrary", "arbitrary"), name="ssd",
    )(zg, xs, bm, cm, dt, dtb, alog, dskip, ng)


def _ffn_kernel(tiles_per_seq, y_ref, wmix_ref, x_ref, g_ref, wup_ref, cw_ref, cb_ref, wdn_ref,
                o_ref, cbuf):
    tm = x_ref.shape[0]
    f = wdn_ref.shape[0]
    halo = SUBLANES
    i = pl.program_id(0)
    first = (i % tiles_per_seq) == 0

    @pl.when(first)
    def _():
        cbuf[0:halo, :] = jnp.zeros((halo, cbuf.shape[1]), F32)

    @pl.when(jnp.logical_not(first))
    def _():
        cbuf[0:halo, :] = cbuf[tm:tm + halo, :]

    x = x_ref[...] + _dot(y_ref[...], wmix_ref[...])
    xn = _rms(x, g_ref[...]).astype(BF16)
    u = _dot(xn, wup_ref[...])
    cbuf[halo:halo + tm, :] = u
    k_w = cw_ref.shape[0]
    conv = cw_ref[k_w - 1:k_w, :] * u + cb_ref[...]
    for s in range(1, k_w):
        conv = conv + cw_ref[k_w - 1 - s:k_w - s, :] * cbuf[halo - s:halo - s + tm, :]
    hid = (_silu(conv[:, :f]) * conv[:, f:]).astype(BF16)
    o_ref[...] = x + _dot(hid, wdn_ref[...])


def _layer_spec(stack, layer):
    nd = stack.ndim - 1
    return pl.BlockSpec((None,) + stack.shape[1:], lambda *_: (layer,) + (0,) * nd,
                        pipeline_mode=pl.Buffered(1))


def _ffn(y, wmix, x, g, wup, cw, cb, wdn, layer, seq, name):
    t, d = x.shape
    tm = FFN_TILE
    return pl.pallas_call(
        functools.partial(_ffn_kernel, seq // tm), grid=(t // tm,),
        in_specs=[pl.BlockSpec((tm, y.shape[1]), lambda i: (i, 0)), _const_spec(wmix.shape),
                  pl.BlockSpec((tm, d), lambda i: (i, 0)), _const_spec((1, d)),
                  _layer_spec(wup, layer), _const_spec(cw.shape), _const_spec((1, cb.shape[0])),
                  _layer_spec(wdn, layer)],
        out_specs=pl.BlockSpec((tm, d), lambda i: (i, 0)),
        out_shape=jax.ShapeDtypeStruct((t, d), F32),
        scratch_shapes=[pltpu.VMEM((tm + SUBLANES, wup.shape[2]), F32)],
        compiler_params=_params("arbitrary"), name=name,
    )(y, wmix, x, g.reshape(1, d), wup, cw, cb.reshape(1, -1), wdn)


def _pair_norm(blk, g2):
    lane = lax.broadcasted_iota(jnp.int32, blk.shape, 1)
    lo = lane < ATT_HEAD_DIM
    sq = blk * blk
    s_lo = jnp.sum(jnp.where(lo, sq, 0.0), axis=-1, keepdims=True)
    s_hi = jnp.sum(jnp.where(lo, 0.0, sq), axis=-1, keepdims=True)
    inv = 1.0 / ATT_HEAD_DIM
    rs = jnp.where(lo, lax.rsqrt(s_lo * inv + EPS), lax.rsqrt(s_hi * inv + EPS))
    return blk * rs * g2


def _qkv_kernel(x_ref, gkv_ref, gq_ref, wkv_ref, wq_ref, kg_ref, qg_ref, k_ref, v_ref, q_ref):
    n_heads = k_ref.shape[0]
    tm = x_ref.shape[0]
    t = q_ref.shape[2] // 2
    x = x_ref[...]
    xr = x * lax.rsqrt(jnp.mean(x * x, axis=-1, keepdims=True) + EPS)
    kv = _dot((xr * gkv_ref[...]).astype(BF16), wkv_ref[...])
    q = _dot((xr * gq_ref[...]).astype(BF16), wq_ref[...])
    kd = n_heads * LANES
    lane = lax.broadcasted_iota(jnp.int32, (tm, LANES), 1)
    lo = lane < ATT_HEAD_DIM
    scale = ATT_HEAD_DIM ** -0.5 * LOG2E
    for h in range(n_heads):
        sl = slice(h * LANES, (h + 1) * LANES)
        k_ref[h] = _pair_norm(kv[:, sl], kg_ref[...]).astype(k_ref.dtype)
        v_ref[h] = kv[:, kd + h * LANES:kd + (h + 1) * LANES].astype(v_ref.dtype)
        qn = _pair_norm(q[:, sl], qg_ref[...]) * scale
        q0 = jnp.where(lo, qn, 0.0).astype(q_ref.dtype)
        q1 = jnp.where(lo, 0.0, qn).astype(q_ref.dtype)
        for qt in range(tm // t):
            q_ref[h, qt, 0:t, :] = q0[qt * t:(qt + 1) * t, :]
            q_ref[h, qt, t:2 * t, :] = q1[qt * t:(qt + 1) * t, :]


def _qkv_proj(x, gkv, gq, wkv, wq, kg2, qg2, n_heads, t):
    tt, d = x.shape
    tm = QKV_TILE
    hm = jax.ShapeDtypeStruct((n_heads, tt, LANES), BF16)
    hspec = pl.BlockSpec((n_heads, tm, LANES), lambda i: (0, i, 0))
    return pl.pallas_call(
        _qkv_kernel, grid=(tt // tm,),
        in_specs=[pl.BlockSpec((tm, d), lambda i: (i, 0)), _const_spec((1, d)), _const_spec((1, d)),
                  _const_spec(wkv.shape), _const_spec(wq.shape), _const_spec((1, LANES)),
                  _const_spec((1, LANES))],
        out_specs=[hspec, hspec, pl.BlockSpec((n_heads, tm // t, 2 * t, LANES), lambda i: (0, i, 0, 0))],
        out_shape=[hm, hm, jax.ShapeDtypeStruct((n_heads, tt // t, 2 * t, LANES), BF16)],
        compiler_params=_params("arbitrary"), name="qkv_proj",
    )(x, gkv.reshape(1, d), gq.reshape(1, d), wkv, wq, kg2, qg2)


def _bucket_thresholds():
    n = np.arange(1, 4 * MAX_DISTANCE, dtype=np.int64)
    max_exact = NUM_BUCKETS // 2
    nf = n.astype(np.float32)
    large = max_exact + (np.log(nf / np.float32(max_exact)) / np.float32(math.log(MAX_DISTANCE / max_exact))
                         * np.float32(NUM_BUCKETS - max_exact)).astype(np.int32)
    large = np.minimum(large, NUM_BUCKETS - 1)
    bucket = np.where(n < max_exact, n, large)
    return [int(n[np.argmax(bucket >= b)]) for b in range(max_exact + 1, NUM_BUCKETS)]


def _bias_kernel(thresholds, rb_ref, o_ref):
    t = o_ref.shape[1]
    h = pl.program_id(0)
    max_exact = NUM_BUCKETS // 2
    row = lax.broadcasted_iota(jnp.int32, (t, t), 0)
    col = lax.broadcasted_iota(jnp.int32, (t, t), 1)
    far = rb_ref[NUM_BUCKETS - 1, h]
    for tiles_left in (1, 0):
        d = row - col + tiles_left * t
        n = jnp.maximum(d, 0)
        bucket = jnp.minimum(n, max_exact)
        for thr in thresholds:
            bucket = bucket + jnp.where(n >= thr, 1, 0)
        val = jnp.zeros((t, t), F32)
        for b in range(NUM_BUCKETS):
            val = jnp.where(bucket == b, (rb_ref[b, h] - far) * LOG2E, val)
        if tiles_left == 0:
            val = jnp.where(d >= 0, val, NEG)
        o_ref[0, :, (1 - tiles_left) * t:(2 - tiles_left) * t] = val


def _bias_tiles(rel_bias, n_heads, t):
    return pl.pallas_call(
        functools.partial(_bias_kernel, _bucket_thresholds()), grid=(n_heads,),
        in_specs=[pl.BlockSpec(memory_space=pltpu.SMEM)],
        out_specs=pl.BlockSpec((1, t, 2 * t), lambda h: (h, 0, 0)),
        out_shape=jax.ShapeDtypeStruct((n_heads, t, 2 * t), F32),
        compiler_params=_params("arbitrary"), name="rel_bias_tiles",
    )(rel_bias)


def _attn_kernel(lam_init, q_scr, lv_ref, sg_ref, k_ref, v_ref, bias_ref, o_ref, m_scr, acc_scr):
    t = o_ref.shape[0]
    n_heads = k_ref.shape[0]
    qi = pl.program_id(1)

    lv = lv_ref[...]
    lam = (jnp.exp(jnp.sum(lv[0:1] * lv[1:2], axis=-1, keepdims=True))
           - jnp.exp(jnp.sum(lv[2:3] * lv[3:4], axis=-1, keepdims=True)) + lam_init)

    def step(j, tiles, bias_col, first=False):
        w = tiles * t
        start = pl.multiple_of(j * t, t)
        ones = jnp.ones((w, LANES), BF16)
        for h in range(n_heads):
            kt = k_ref[h, pl.ds(start, w), :]
            vx = jnp.concatenate([v_ref[h, pl.ds(start, w), :], ones], axis=1)
            for r0 in range(0, 2 * t, ATT_ROWS):
                rows = slice(r0, r0 + ATT_ROWS)
                s = _dot_nt(q_scr[h, rows, :], kt)
                if bias_col is not None:
                    b0 = r0 % t
                    s = s + bias_ref[h, b0:b0 + ATT_ROWS, bias_col:bias_col + w]
                m_cur = jnp.max(s, axis=-1, keepdims=True)
                if first:
                    m_new = jnp.broadcast_to(m_cur, (ATT_ROWS, LANES))
                else:
                    m_prev = m_scr[h, rows, :]
                    m_new = jnp.maximum(m_prev, m_cur)
                p = jnp.exp2(s - jnp.concatenate([m_new] * (w // LANES), axis=1)).astype(BF16)
                pv = _dot(p, vx)
                if not first:
                    alpha = jnp.exp2(m_prev - m_new)
                    pv = jnp.concatenate([alpha, alpha], axis=1) * acc_scr[h, rows, :] + pv
                acc_scr[h, rows, :] = pv
                m_scr[h, rows, :] = m_new

    @pl.when(qi == 0)
    def _():
        step(0, 1, t, first=True)

    @pl.when(qi >= 1)
    def _():
        step(qi - 1, 2, 0, first=True)

    n_far = jnp.maximum(qi - 1, 0)

    def far_step(jj, c):
        step(2 * jj, 2, None)
        return c

    lax.fori_loop(0, n_far // 2, far_step, 0)

    @pl.when(n_far % 2 == 1)
    def _():
        step(n_far - 1, 1, None)

    for h in range(n_heads):
        acc = acc_scr[h]
        o = acc[:, :LANES] / acc[:, LANES:]
        d = o[:t] - lam * o[t:]
        d = _rms(d, sg_ref[...]) * (1.0 - lam_init)
        o_ref[:, h * LANES:(h + 1) * LANES] = d.astype(o_ref.dtype)


def _attention(qs, lam_vecs, sg, k, v, bias, lam_init, batch, seq):
    n_heads, tt, _ = k.shape
    t = ATT_TILE
    nq = seq // t
    kd = n_heads * LANES
    kv_spec = pl.BlockSpec((n_heads, seq, LANES), lambda b, i: (0, b, 0))
    return pl.pallas_call(
        functools.partial(_attn_kernel, lam_init), grid=(batch, nq),
        in_specs=[pl.BlockSpec((n_heads, None, 2 * t, LANES), lambda b, i: (0, b * nq + i, 0, 0)),
                  _const_spec(lam_vecs.shape), _const_spec((1, LANES)), kv_spec, kv_spec,
                  _const_spec(bias.shape)],
        out_specs=pl.BlockSpec((t, kd), lambda b, i: (b * nq + i, 0)),
        out_shape=jax.ShapeDtypeStruct((tt, kd), BF16),
        scratch_shapes=[pltpu.VMEM((n_heads, 2 * t, LANES), F32),
                        pltpu.VMEM((n_heads, 2 * t, 2 * LANES), F32)],
        compiler_params=_params("arbitrary", "arbitrary"), name="diff_attention",
    )(qs, lam_vecs, sg.reshape(1, LANES), k, v, bias)


def kernel(x, ssm_ln_g, ssm_in_w, ssm_conv_w, ssm_conv_b, ssm_dt_bias, ssm_a_log, ssm_d, ssm_norm_g, ssm_out_w, kv_ln_g, kv_w, k_norm_g, rel_bias, attn_ln_g, q_w, q_norm_g, lam_vecs, subln_g, attn_out_w, ffn_ln_g, ffn_up_w, ffn_conv_w, ffn_conv_b, ffn_down_w):
    batch, seq, d = x.shape
    t = batch * seq
    n_a = ssm_in_w.shape[0]
    depth = ffn_up_w.shape[0]
    h = x.reshape(t, d)
    k_sh = v_sh = bias = None
    n_att_heads = q_w.shape[2] // LANES
    ffn_up = ffn_up_w.astype(BF16)
    ffn_down = ffn_down_w.astype(BF16)

    for layer in range(depth):
        if layer < n_a:
            i = layer
            d_inner = ssm_out_w.shape[1]
            n_ssm_heads = ssm_dt_bias.shape[1]
            conv_dim = ssm_conv_w.shape[2]
            w_dt = jnp.pad(ssm_in_w[i][:, d_inner + conv_dim:].astype(BF16),
                           ((0, 0), (0, LANES - n_ssm_heads)))
            zg, xs, bm, cm, dt = _in_proj(h, ssm_ln_g[i], ssm_in_w[i].astype(BF16), w_dt, ssm_conv_w[i],
                                          ssm_conv_b[i].reshape(1, -1), d_inner, seq)
            pad_h = ((0, 0), (0, LANES - n_ssm_heads))
            mix = _ssd(zg, xs, bm, cm, dt, jnp.pad(ssm_dt_bias[i].reshape(1, -1), pad_h),
                       jnp.pad(ssm_a_log[i].reshape(1, -1), pad_h),
                       ssm_d[i], jnp.broadcast_to(ssm_norm_g[i][:, None], (d_inner, LANES)), batch, seq)
            w_mix = ssm_out_w[i].astype(BF16)
        else:
            j = layer - n_a
            if j == 0:
                bias = _bias_tiles(rel_bias, n_att_heads, ATT_TILE)
            k_new, v_new, qs = _qkv_proj(h, kv_ln_g, attn_ln_g[j], kv_w.astype(BF16), q_w[j].astype(BF16),
                                         jnp.tile(k_norm_g, 2).reshape(1, LANES),
                                         jnp.tile(q_norm_g[j], 2).reshape(1, LANES), n_att_heads, ATT_TILE)
            if j == 0:
                k_sh, v_sh = k_new, v_new
            lam_init = 0.8 - 0.6 * math.exp(-0.3 * layer)
            mix = _attention(qs, lam_vecs[j], subln_g[j], k_sh, v_sh, bias, lam_init, batch, seq)
            w_mix = attn_out_w[j].astype(BF16)
        h = _ffn(mix, w_mix, h, ffn_ln_g[layer], ffn_up, ffn_conv_w[layer], ffn_conv_b[layer],
                 ffn_down, layer, seq, f"conv_ffn_{layer}")
    return h.reshape(batch, seq, d)
```

```python
import functools
import math

import numpy as np
import jax
import jax.numpy as jnp
from jax import lax
from jax.experimental import pallas as pl
from jax.experimental.pallas import tpu as pltpu

F32 = jnp.float32
BF16 = jnp.bfloat16

EPS = 1e-6
NEG = -1e30
LOG2E = math.log2(math.e)

SSM_GROUPS = 4
SSM_STATE = 128
SSM_HEAD_DIM = 64
SSM_CHUNK = 128
ATT_HEAD_DIM = 64
NUM_BUCKETS = 32
MAX_DISTANCE = 128
LANES = 128
SUBLANES = 8
VMEM_LIMIT = 56 * 1024 * 1024

ROW_TILE = 512
QKV_TILE = 1024
FFN_TILE = 512
ATT_TILE = 256
ATT_ROWS = 128


def _params(*sem):
    return pltpu.CompilerParams(dimension_semantics=sem, vmem_limit_bytes=VMEM_LIMIT)


def _const_spec(shape):
    nd = len(shape)
    return pl.BlockSpec(shape, lambda *_: (0,) * nd, pipeline_mode=pl.Buffered(1))


def _rms(x, g):
    ms = jnp.mean(x * x, axis=-1, keepdims=True)
    return x * lax.rsqrt(ms + EPS) * g


def _silu(x):
    h = 0.5 * x
    return h * jnp.tanh(h) + h


def _split3(v):
    hi = v.astype(BF16)
    r1 = v - hi.astype(F32)
    mid = r1.astype(BF16)
    lo = (r1 - mid.astype(F32)).astype(BF16)
    return hi, mid, lo


def _dot(a, b):
    return jnp.dot(a, b, preferred_element_type=F32)


def _dot_nt(a, b):
    return lax.dot_general(a, b, (((1,), (1,)), ((), ())), preferred_element_type=F32)


def _in_proj_kernel(tiles_per_seq, x_ref, g_ref, w_ref, wdt_ref, cw_ref, cb_ref,
                    zg_ref, xs_ref, b_ref, c_ref, dt_ref, ctail):
    tm = x_ref.shape[0]
    l = SSM_CHUNK
    d_inner = xs_ref.shape[1]
    conv_dim = cw_ref.shape[1]
    gn = b_ref.shape[1]
    tail = ctail.shape[0]

    @pl.when(pl.program_id(0) % tiles_per_seq == 0)
    def _():
        ctail[...] = jnp.zeros(ctail.shape, F32)

    xn = _interleave_rows(_rms(x_ref[...], g_ref[...]).astype(BF16))
    xbc = _dot(xn, w_ref[:, d_inner:d_inner + conv_dim])
    zg = _silu(_dot(xn, w_ref[:, :d_inner]))
    for c0 in range(0, tm, l):
        zg_ref[c0 // l] = zg[c0:c0 + l, :].T
    dt_ref[...] = _dot(xn, wdt_ref[...])
    cw_half = 0.5 * cw_ref[...]
    cb_half = 0.5 * cb_ref[...]
    prev = ctail[...]
    for c0 in range(0, tm, l):
        u = xbc[c0:c0 + l, :]
        half = _interleaved_conv(u, prev, cw_half, cb_half)
        act = half * jnp.tanh(half) + half
        xs_ref[c0 // l] = act[:, :d_inner].T
        b_ref[c0:c0 + l, :] = act[:, d_inner:d_inner + gn].astype(b_ref.dtype)
        c_ref[c0:c0 + l, :] = act[:, d_inner + gn:].astype(c_ref.dtype)
        prev = u[l - tail:l, :]
    ctail[...] = prev


def _in_proj(x, g, w, wdt, cw, cb, d_inner, seq):
    t, d = x.shape
    tm = ROW_TILE
    l = SSM_CHUNK
    conv_dim = cw.shape[1]
    gn = (conv_dim - d_inner) // 2
    rows = lambda n, dt: (pl.BlockSpec((tm, n), lambda i: (i, 0)), jax.ShapeDtypeStruct((t, n), dt))
    chunked = (pl.BlockSpec((tm // l, d_inner, l), lambda i: (i, 0, 0)),
               jax.ShapeDtypeStruct((t // l, d_inner, l), F32))
    outs = (chunked, chunked, rows(gn, BF16), rows(gn, BF16), rows(LANES, F32))
    return pl.pallas_call(
        functools.partial(_in_proj_kernel, seq // tm), grid=(t // tm,),
        in_specs=[pl.BlockSpec((tm, d), lambda i: (i, 0)), _const_spec((1, d)),
                  _const_spec(w.shape), _const_spec(wdt.shape), _const_spec(cw.shape),
                  _const_spec(cb.shape)],
        out_specs=[spec for spec, _ in outs],
        out_shape=[shape for _, shape in outs],
        scratch_shapes=[pltpu.VMEM(((cw.shape[0] - 1) * SUBLANES, conv_dim), F32)],
        compiler_params=_params("arbitrary"), name="ssm_in_proj",
    )(x, g.reshape(1, d), w, wdt, cw, cb)


GROUPS_PER_CHUNK = SSM_CHUNK // SUBLANES
SSD_CHUNKS_PER_STEP = 4


def _chunk_time(r):
    return r // SUBLANES + GROUPS_PER_CHUNK * (r % SUBLANES)


def _interleave_rows(v, inverse=False):
    l = SSM_CHUNK
    r = lax.broadcasted_iota(jnp.int32, (l, l), 0)
    c = lax.broadcasted_iota(jnp.int32, (l, l), 1)
    hit = (r == _chunk_time(c)) if inverse else (c == _chunk_time(r))
    perm = jnp.where(hit, 1.0, 0.0).astype(BF16)
    out = [_dot(perm, v[b:b + l, :]).astype(BF16) for b in range(0, v.shape[0], l)]
    return out[0] if len(out) == 1 else jnp.concatenate(out, axis=0)


def _interleaved_conv(u, prev, cw, cb):
    l = u.shape[0]
    k_w = cw.shape[0]
    tail = prev.shape[0]
    sub8 = lax.broadcasted_iota(jnp.int32, (SUBLANES, u.shape[1]), 0)
    out = cw[k_w - 1:k_w, :] * u + cb
    for s in range(1, k_w):
        fixed = []
        for i in range(s):
            r0 = l - (s - i) * SUBLANES
            p0 = tail - (s - i) * SUBLANES
            fixed.append(jnp.where(sub8 == 0, pltpu.roll(prev[p0:p0 + SUBLANES, :], 1, 0),
                                   pltpu.roll(u[r0:r0 + SUBLANES, :], 1, 0)))
        delayed = jnp.concatenate(fixed + [u[0:l - s * SUBLANES, :]], axis=0)
        out = out + cw[k_w - 1 - s:k_w - s, :] * delayed
    return out


def _ssd_kernel(zg_ref, xs_ref, b_ref, c_ref, dt_ref, dtb_ref, alog_ref, dskip_ref, ng_ref,
                o_ref, state, yt_scr):
    l = SSM_CHUNK
    d_inner = xs_ref.shape[1]
    n_heads = d_inner // SSM_HEAD_DIM
    heads_per_group = n_heads // SSM_GROUPS
    gw = d_inner // SSM_GROUPS

    @pl.when(pl.program_id(1) == 0)
    def _():
        state[...] = jnp.zeros(state.shape, F32)

    for ci in range(xs_ref.shape[0]):
        rows = slice(ci * l, (ci + 1) * l)
        xs_t = xs_ref[ci]
        bm = b_ref[rows, :]
        cm = c_ref[rows, :]

        pre = dt_ref[rows, :] + dtb_ref[...]
        dtv = jnp.maximum(pre, 0.0) + jnp.log(1.0 + jnp.exp(-jnp.abs(pre)))
        a = dtv * (-jnp.exp(alog_ref[...]))

        t_row = _chunk_time(lax.broadcasted_iota(jnp.int32, (l, l), 0))
        t_col = _chunk_time(lax.broadcasted_iota(jnp.int32, (l, l), 1))
        tri = jnp.where(t_row >= t_col, 1.0, 0.0).astype(BF16)
        a_hi, a_mid, a_lo = _split3(a)
        cs = (_dot(tri, a_hi) + _dot(tri, a_mid) + _dot(tri, a_lo)) * LOG2E
        cml = cs - jnp.log(dtv) * LOG2E

        cs_t = cs.T
        dt_t = dtv.T
        tot = cs_t[:, l - 1:l]
        w_t = dt_t * jnp.exp2(tot - cs_t)
        g_t = jnp.broadcast_to(jnp.exp2(tot), (LANES, l))
        e_t = jnp.exp2(cs_t)
        later = t_col >= t_row

        for g in range(SSM_GROUPS):
            b_g = bm[:, g * SSM_STATE:(g + 1) * SSM_STATE]
            c_g = cm[:, g * SSM_STATE:(g + 1) * SSM_STATE]
            cbt = _dot_nt(b_g, c_g)
            yoff = _dot_nt(state[g].astype(BF16), c_g)
            for hl in range(heads_per_group):
                h = g * heads_per_group + hl
                ch = slice(h * SSM_HEAD_DIM, (h + 1) * SSM_HEAD_DIM)
                gh = slice(hl * SSM_HEAD_DIM, (hl + 1) * SSM_HEAD_DIM)
                x_h = xs_t[ch, :]
                colv = jnp.broadcast_to(cml[:, h:h + 1], (l, l))
                rowv = jnp.broadcast_to(cs_t[h:h + 1, :], (l, l))
                mt = (cbt * jnp.exp2(jnp.where(later, rowv - colv, NEG))).astype(BF16)
                y_h = _dot(x_h.astype(BF16), mt) + yoff[gh, :] * e_t[h:h + 1, :] + x_h * dskip_ref[h]
                yt_scr[ci, ch, :] = y_h
                d_s = _dot((x_h * w_t[h:h + 1, :]).astype(BF16), b_g)
                state[g, gh, :] = g_t[h:h + 1, :] * state[g, gh, :] + d_s

        r_i = lax.broadcasted_iota(jnp.int32, (l, l), 0)
        c_i = lax.broadcasted_iota(jnp.int32, (l, l), 1)
        restore = jnp.where(r_i == _chunk_time(c_i), 1.0, 0.0).astype(BF16)
        for g in range(SSM_GROUPS):
            sl = slice(g * gw, (g + 1) * gw)
            yt = yt_scr[ci, sl, :] * zg_ref[ci, sl, :]
            ms = jnp.mean(yt * yt, axis=0, keepdims=True)
            yn = (yt * lax.rsqrt(ms + EPS) * ng_ref[sl, :]).astype(BF16)
            o_ref[rows, sl] = _dot_nt(restore, yn).astype(o_ref.dtype)


def _ssd(zg, xs, bm, cm, dt, dtb, alog, dskip, ng, batch, seq):
    d_inner = xs.shape[1]
    t, gn = bm.shape
    n = SSD_CHUNKS_PER_STEP
    l = SSM_CHUNK
    ns = seq // (n * l)
    row = lambda b, c: (b * ns + c, 0)
    return pl.pallas_call(
        _ssd_kernel, grid=(batch, ns),
        in_specs=[pl.BlockSpec((n, d_inner, l), lambda b, c: (b * ns + c, 0, 0)),
                  pl.BlockSpec((n, d_inner, l), lambda b, c: (b * ns + c, 0, 0)),
                  pl.BlockSpec((n * l, gn), row), pl.BlockSpec((n * l, gn), row),
                  pl.BlockSpec((n * l, LANES), row),
                  _const_spec(dtb.shape), _const_spec(alog.shape),
                  pl.BlockSpec(memory_space=pltpu.SMEM), _const_spec(ng.shape)],
        out_specs=pl.BlockSpec((n * l, d_inner), row),
        out_shape=jax.ShapeDtypeStruct((t, d_inner), BF16),
        scratch_shapes=[pltpu.VMEM((SSM_GROUPS, d_inner // SSM_GROUPS, SSM_STATE), F32),
                        pltpu.VMEM((n, d_inner, l), F32)],
        compiler_params=_params("arbitrary", "arbitrary"), name="ssd",
    )(zg, xs, bm, cm, dt, dtb, alog, dskip, ng)


def _ffn_kernel(tiles_per_seq, y_ref, wmix_ref, x_ref, g_ref, wup_ref, cw_ref, cb_ref, wdn_ref,
                o_ref, ctail):
    tm = x_ref.shape[0]
    f = wdn_ref.shape[0]

    @pl.when(pl.program_id(0) % tiles_per_seq == 0)
    def _():
        ctail[...] = jnp.zeros(ctail.shape, F32)

    x = x_ref[...] + _dot(y_ref[...], wmix_ref[...])
    xn = _rms(x, g_ref[...]).astype(BF16)
    u = _dot(xn, wup_ref[...])
    prev = ctail[...]
    sub8 = lax.broadcasted_iota(jnp.int32, (SUBLANES, u.shape[1]), 0)
    k_w = cw_ref.shape[0]
    conv = cw_ref[k_w - 1:k_w, :] * u + cb_ref[...]
    for s in range(1, k_w):
        rolled = pltpu.roll(u, s, 0)
        head = jnp.where(sub8 < s, pltpu.roll(prev, s, 0), rolled[:SUBLANES, :])
        delayed = jnp.concatenate([head, rolled[SUBLANES:, :]], axis=0)
        conv = conv + cw_ref[k_w - 1 - s:k_w - s, :] * delayed
    ctail[...] = u[tm - SUBLANES:tm, :]
    hid = (_silu(conv[:, :f]) * conv[:, f:]).astype(BF16)
    o_ref[...] = x + _dot(hid, wdn_ref[...])


def _layer_spec(stack, layer):
    nd = stack.ndim - 1
    return pl.BlockSpec((None,) + stack.shape[1:], lambda *_: (layer,) + (0,) * nd,
                        pipeline_mode=pl.Buffered(1))


def _ffn(y, wmix, x, g, wup, cw, cb, wdn, layer, seq, name):
    t, d = x.shape
    tm = FFN_TILE
    return pl.pallas_call(
        functools.partial(_ffn_kernel, seq // tm), grid=(t // tm,),
        in_specs=[pl.BlockSpec((tm, y.shape[1]), lambda i: (i, 0)), _const_spec(wmix.shape),
                  pl.BlockSpec((tm, d), lambda i: (i, 0)), _const_spec((1, d)),
                  _layer_spec(wup, layer), _const_spec(cw.shape), _const_spec((1, cb.shape[0])),
                  _layer_spec(wdn, layer)],
        out_specs=pl.BlockSpec((tm, d), lambda i: (i, 0)),
        out_shape=jax.ShapeDtypeStruct((t, d), F32),
        scratch_shapes=[pltpu.VMEM((SUBLANES, wup.shape[2]), F32)],
        compiler_params=_params("arbitrary"), name=name,
    )(y, wmix, x, g.reshape(1, d), wup, cw, cb.reshape(1, -1), wdn)


def _pair_norm(blk, g2):
    lane = lax.broadcasted_iota(jnp.int32, blk.shape, 1)
    lo = lane < ATT_HEAD_DIM
    sq = blk * blk
    s_lo = jnp.sum(jnp.where(lo, sq, 0.0), axis=-1, keepdims=True)
    s_hi = jnp.sum(jnp.where(lo, 0.0, sq), axis=-1, keepdims=True)
    inv = 1.0 / ATT_HEAD_DIM
    rs = jnp.where(lo, lax.rsqrt(s_lo * inv + EPS), lax.rsqrt(s_hi * inv + EPS))
    return blk * rs * g2


def _qkv_kernel(x_ref, gkv_ref, gq_ref, wkv_ref, wq_ref, kg_ref, qg_ref, k_ref, v_ref, q_ref):
    n_heads = k_ref.shape[0]
    tm = x_ref.shape[0]
    t = q_ref.shape[2] // 2
    x = x_ref[...]
    xr = x * lax.rsqrt(jnp.mean(x * x, axis=-1, keepdims=True) + EPS)
    kv = _dot((xr * gkv_ref[...]).astype(BF16), wkv_ref[...])
    q = _dot((xr * gq_ref[...]).astype(BF16), wq_ref[...])
    kd = n_heads * LANES
    lane = lax.broadcasted_iota(jnp.int32, (tm, LANES), 1)
    lo = lane < ATT_HEAD_DIM
    scale = ATT_HEAD_DIM ** -0.5 * LOG2E
    for h in range(n_heads):
        sl = slice(h * LANES, (h + 1) * LANES)
        k_ref[h] = _pair_norm(kv[:, sl], kg_ref[...]).astype(k_ref.dtype)
        v_ref[h] = kv[:, kd + h * LANES:kd + (h + 1) * LANES].astype(v_ref.dtype)
        qn = _pair_norm(q[:, sl], qg_ref[...]) * scale
        q0 = jnp.where(lo, qn, 0.0).astype(q_ref.dtype)
        q1 = jnp.where(lo, 0.0, qn).astype(q_ref.dtype)
        for qt in range(tm // t):
            q_ref[h, qt, 0:t, :] = q0[qt * t:(qt + 1) * t, :]
            q_ref[h, qt, t:2 * t, :] = q1[qt * t:(qt + 1) * t, :]


def _qkv_proj(x, gkv, gq, wkv, wq, kg2, qg2, n_heads, t):
    tt, d = x.shape
    tm = QKV_TILE
    hm = jax.ShapeDtypeStruct((n_heads, tt, LANES), BF16)
    hspec = pl.BlockSpec((n_heads, tm, LANES), lambda i: (0, i, 0))
    return pl.pallas_call(
        _qkv_kernel, grid=(tt // tm,),
        in_specs=[pl.BlockSpec((tm, d), lambda i: (i, 0)), _const_spec((1, d)), _const_spec((1, d)),
                  _const_spec(wkv.shape), _const_spec(wq.shape), _const_spec((1, LANES)),
                  _const_spec((1, LANES))],
        out_specs=[hspec, hspec, pl.BlockSpec((n_heads, tm // t, 2 * t, LANES), lambda i: (0, i, 0, 0))],
        out_shape=[hm, hm, jax.ShapeDtypeStruct((n_heads, tt // t, 2 * t, LANES), BF16)],
        compiler_params=_params("arbitrary"), name="qkv_proj",
    )(x, gkv.reshape(1, d), gq.reshape(1, d), wkv, wq, kg2, qg2)


def _bucket_thresholds():
    n = np.arange(1, 4 * MAX_DISTANCE, dtype=np.int64)
    max_exact = NUM_BUCKETS // 2
    nf = n.astype(np.float32)
    large = max_exact + (np.log(nf / np.float32(max_exact)) / np.float32(math.log(MAX_DISTANCE / max_exact))
                         * np.float32(NUM_BUCKETS - max_exact)).astype(np.int32)
    large = np.minimum(large, NUM_BUCKETS - 1)
    bucket = np.where(n < max_exact, n, large)
    return [int(n[np.argmax(bucket >= b)]) for b in range(max_exact + 1, NUM_BUCKETS)]


def _bias_kernel(thresholds, rb_ref, o_ref):
    t = o_ref.shape[1]
    h = pl.program_id(0)
    max_exact = NUM_BUCKETS // 2
    row = lax.broadcasted_iota(jnp.int32, (t, t), 0)
    col = lax.broadcasted_iota(jnp.int32, (t, t), 1)
    far = rb_ref[NUM_BUCKETS - 1, h]
    for tiles_left in (1, 0):
        d = row - col + tiles_left * t
        n = jnp.maximum(d, 0)
        bucket = jnp.minimum(n, max_exact)
        for thr in thresholds:
            bucket = bucket + jnp.where(n >= thr, 1, 0)
        val = jnp.zeros((t, t), F32)
        for b in range(NUM_BUCKETS):
            val = jnp.where(bucket == b, (rb_ref[b, h] - far) * LOG2E, val)
        if tiles_left == 0:
            val = jnp.where(d >= 0, val, NEG)
        o_ref[0, :, (1 - tiles_left) * t:(2 - tiles_left) * t] = val


def _bias_tiles(rel_bias, n_heads, t):
    return pl.pallas_call(
        functools.partial(_bias_kernel, _bucket_thresholds()), grid=(n_heads,),
        in_specs=[pl.BlockSpec(memory_space=pltpu.SMEM)],
        out_specs=pl.BlockSpec((1, t, 2 * t), lambda h: (h, 0, 0)),
        out_shape=jax.ShapeDtypeStruct((n_heads, t, 2 * t), F32),
        compiler_params=_params("arbitrary"), name="rel_bias_tiles",
    )(rel_bias)


def _attn_kernel(lam_init, q_scr, lv_ref, sg_ref, k_ref, v_ref, bias_ref, o_ref, m_scr, acc_scr):
    t = o_ref.shape[0]
    n_heads = k_ref.shape[0]
    qi = pl.program_id(1)

    lv = lv_ref[...]
    lam = (jnp.exp(jnp.sum(lv[0:1] * lv[1:2], axis=-1, keepdims=True))
           - jnp.exp(jnp.sum(lv[2:3] * lv[3:4], axis=-1, keepdims=True)) + lam_init)

    def step(j, tiles, bias_col, first=False):
        w = tiles * t
        start = pl.multiple_of(j * t, t)
        ones = jnp.ones((w, LANES), BF16)
        for h in range(n_heads):
            kt = k_ref[h, pl.ds(start, w), :]
            vx = jnp.concatenate([v_ref[h, pl.ds(start, w), :], ones], axis=1)
            for r0 in range(0, 2 * t, ATT_ROWS):
                rows = slice(r0, r0 + ATT_ROWS)
                s = _dot_nt(q_scr[h, rows, :], kt)
                if bias_col is not None:
                    b0 = r0 % t
                    s = s + bias_ref[h, b0:b0 + ATT_ROWS, bias_col:bias_col + w]
                m_cur = jnp.max(s, axis=-1, keepdims=True)
                if first:
                    m_new = jnp.broadcast_to(m_cur, (ATT_ROWS, LANES))
                else:
                    m_prev = m_scr[h, rows, :]
                    m_new = jnp.maximum(m_prev, m_cur)
                p = jnp.exp2(s - jnp.concatenate([m_new] * (w // LANES), axis=1)).astype(BF16)
                pv = _dot(p, vx)
                if not first:
                    alpha = jnp.exp2(m_prev - m_new)
                    pv = jnp.concatenate([alpha, alpha], axis=1) * acc_scr[h, rows, :] + pv
                acc_scr[h, rows, :] = pv
                m_scr[h, rows, :] = m_new

    @pl.when(qi == 0)
    def _():
        step(0, 1, t, first=True)

    @pl.when(qi >= 1)
    def _():
        step(qi - 1, 2, 0, first=True)

    n_far = jnp.maximum(qi - 1, 0)

    def far_step(jj, c):
        step(2 * jj, 2, None)
        return c

    lax.fori_loop(0, n_far // 2, far_step, 0)

    @pl.when(n_far % 2 == 1)
    def _():
        step(n_far - 1, 1, None)

    for h in range(n_heads):
        acc = acc_scr[h]
        o = acc[:, :LANES] / acc[:, LANES:]
        d = o[:t] - lam * o[t:]
        d = _rms(d, sg_ref[...]) * (1.0 - lam_init)
        o_ref[:, h * LANES:(h + 1) * LANES] = d.astype(o_ref.dtype)


def _attention(qs, lam_vecs, sg, k, v, bias, lam_init, batch, seq):
    n_heads, tt, _ = k.shape
    t = ATT_TILE
    nq = seq // t
    kd = n_heads * LANES
    kv_spec = pl.BlockSpec((n_heads, seq, LANES), lambda b, i: (0, b, 0))
    return pl.pallas_call(
        functools.partial(_attn_kernel, lam_init), grid=(batch, nq),
        in_specs=[pl.BlockSpec((n_heads, None, 2 * t, LANES), lambda b, i: (0, b * nq + i, 0, 0)),
                  _const_spec(lam_vecs.shape), _const_spec((1, LANES)), kv_spec, kv_spec,
                  _const_spec(bias.shape)],
        out_specs=pl.BlockSpec((t, kd), lambda b, i: (b * nq + i, 0)),
        out_shape=jax.ShapeDtypeStruct((tt, kd), BF16),
        scratch_shapes=[pltpu.VMEM((n_heads, 2 * t, LANES), F32),
                        pltpu.VMEM((n_heads, 2 * t, 2 * LANES), F32)],
        compiler_params=_params("arbitrary", "arbitrary"), name="diff_attention",
    )(qs, lam_vecs, sg.reshape(1, LANES), k, v, bias)


def kernel(x, ssm_ln_g, ssm_in_w, ssm_conv_w, ssm_conv_b, ssm_dt_bias, ssm_a_log, ssm_d, ssm_norm_g, ssm_out_w, kv_ln_g, kv_w, k_norm_g, rel_bias, attn_ln_g, q_w, q_norm_g, lam_vecs, subln_g, attn_out_w, ffn_ln_g, ffn_up_w, ffn_conv_w, ffn_conv_b, ffn_down_w):
    batch, seq, d = x.shape
    t = batch * seq
    n_a = ssm_in_w.shape[0]
    depth = ffn_up_w.shape[0]
    h = x.reshape(t, d)
    k_sh = v_sh = bias = None
    n_att_heads = q_w.shape[2] // LANES
    ffn_up = ffn_up_w.astype(BF16)
    ffn_down = ffn_down_w.astype(BF16)

    for layer in range(depth):
        if layer < n_a:
            i = layer
            d_inner = ssm_out_w.shape[1]
            n_ssm_heads = ssm_dt_bias.shape[1]
            conv_dim = ssm_conv_w.shape[2]
            w_dt = jnp.pad(ssm_in_w[i][:, d_inner + conv_dim:].astype(BF16),
                           ((0, 0), (0, LANES - n_ssm_heads)))
            zg, xs, bm, cm, dt = _in_proj(h, ssm_ln_g[i], ssm_in_w[i].astype(BF16), w_dt, ssm_conv_w[i],
                                          ssm_conv_b[i].reshape(1, -1), d_inner, seq)
            pad_h = ((0, 0), (0, LANES - n_ssm_heads))
            mix = _ssd(zg, xs, bm, cm, dt, jnp.pad(ssm_dt_bias[i].reshape(1, -1), pad_h),
                       jnp.pad(ssm_a_log[i].reshape(1, -1), pad_h),
                       ssm_d[i], jnp.broadcast_to(ssm_norm_g[i][:, None], (d_inner, LANES)), batch, seq)
            w_mix = ssm_out_w[i].astype(BF16)
        else:
            j = layer - n_a
            if j == 0:
                bias = _bias_tiles(rel_bias, n_att_heads, ATT_TILE)
            k_new, v_new, qs = _qkv_proj(h, kv_ln_g, attn_ln_g[j], kv_w.astype(BF16), q_w[j].astype(BF16),
                                         jnp.tile(k_norm_g, 2).reshape(1, LANES),
                                         jnp.tile(q_norm_g[j], 2).reshape(1, LANES), n_att_heads, ATT_TILE)
            if j == 0:
                k_sh, v_sh = k_new, v_new
            lam_init = 0.8 - 0.6 * math.exp(-0.3 * layer)
            mix = _attention(qs, lam_vecs[j], subln_g[j], k_sh, v_sh, bias, lam_init, batch, seq)
            w_mix = attn_out_w[j].astype(BF16)
        h = _ffn(mix, w_mix, h, ffn_ln_g[layer], ffn_up, ffn_conv_w[layer], ffn_conv_b[layer],
                 ffn_down, layer, seq, f"conv_ffn_{layer}")
    return h.reshape(batch, seq, d)
```

```python
import functools
import math

import numpy as np
import jax
import jax.numpy as jnp
from jax import lax
from jax.experimental import pallas as pl
from jax.experimental.pallas import tpu as pltpu

F32 = jnp.float32
BF16 = jnp.bfloat16

EPS = 1e-6
NEG = -1e30
LOG2E = math.log2(math.e)

SSM_GROUPS = 4
SSM_STATE = 128
SSM_HEAD_DIM = 64
SSM_CHUNK = 128
ATT_HEAD_DIM = 64
NUM_BUCKETS = 32
MAX_DISTANCE = 128
LANES = 128
SUBLANES = 8
VMEM_LIMIT = 56 * 1024 * 1024

ROW_TILE = 512
QKV_TILE = 1024
FFN_TILE = 512
ATT_TILE = 256
ATT_ROWS = 128


def _params(*sem):
    return pltpu.CompilerParams(dimension_semantics=sem, vmem_limit_bytes=VMEM_LIMIT)


def _const_spec(shape):
    nd = len(shape)
    return pl.BlockSpec(shape, lambda *_: (0,) * nd, pipeline_mode=pl.Buffered(1))


def _rms(x, g):
    ms = jnp.mean(x * x, axis=-1, keepdims=True)
    return x * lax.rsqrt(ms + EPS) * g


def _silu(x):
    h = 0.5 * x
    return h * jnp.tanh(h) + h


def _split3(v):
    hi = v.astype(BF16)
    r1 = v - hi.astype(F32)
    mid = r1.astype(BF16)
    lo = (r1 - mid.astype(F32)).astype(BF16)
    return hi, mid, lo


def _dot(a, b):
    return jnp.dot(a, b, preferred_element_type=F32)


def _dot_nt(a, b):
    return lax.dot_general(a, b, (((1,), (1,)), ((), ())), preferred_element_type=F32)


def _in_proj_kernel(tiles_per_seq, x_ref, g_ref, w_ref, wdt_ref, cw_ref, cb_ref,
                    zg_ref, xs_ref, b_ref, c_ref, dt_ref, ctail):
    tm = x_ref.shape[0]
    l = SSM_CHUNK
    d_inner = xs_ref.shape[1]
    conv_dim = cw_ref.shape[1]
    gn = b_ref.shape[1]
    tail = ctail.shape[0]

    @pl.when(pl.program_id(0) % tiles_per_seq == 0)
    def _():
        ctail[...] = jnp.zeros(ctail.shape, F32)

    xn = _interleave_rows(_rms(x_ref[...], g_ref[...]).astype(BF16))
    xbc = _dot(xn, w_ref[:, d_inner:d_inner + conv_dim])
    zg = _silu(_dot(xn, w_ref[:, :d_inner]))
    for c0 in range(0, tm, l):
        zg_ref[c0 // l] = zg[c0:c0 + l, :].T
    dt_ref[...] = _dot(xn, wdt_ref[...])
    cw_half = 0.5 * cw_ref[...]
    cb_half = 0.5 * cb_ref[...]
    prev = ctail[...]
    for c0 in range(0, tm, l):
        u = xbc[c0:c0 + l, :]
        half = _interleaved_conv(u, prev, cw_half, cb_half)
        act = half * jnp.tanh(half) + half
        xs_ref[c0 // l] = act[:, :d_inner].T
        b_ref[c0:c0 + l, :] = act[:, d_inner:d_inner + gn].astype(b_ref.dtype)
        c_ref[c0:c0 + l, :] = act[:, d_inner + gn:].astype(c_ref.dtype)
        prev = u[l - tail:l, :]
    ctail[...] = prev


def _in_proj(x, g, w, wdt, cw, cb, d_inner, seq):
    t, d = x.shape
    tm = ROW_TILE
    l = SSM_CHUNK
    conv_dim = cw.shape[1]
    gn = (conv_dim - d_inner) // 2
    rows = lambda n, dt: (pl.BlockSpec((tm, n), lambda i: (i, 0)), jax.ShapeDtypeStruct((t, n), dt))
    chunked = (pl.BlockSpec((tm // l, d_inner, l), lambda i: (i, 0, 0)),
               jax.ShapeDtypeStruct((t // l, d_inner, l), F32))
    outs = (chunked, chunked, rows(gn, BF16), rows(gn, BF16), rows(LANES, F32))
    return pl.pallas_call(
        functools.partial(_in_proj_kernel, seq // tm), grid=(t // tm,),
        in_specs=[pl.BlockSpec((tm, d), lambda i: (i, 0)), _const_spec((1, d)),
                  _const_spec(w.shape), _const_spec(wdt.shape), _const_spec(cw.shape),
                  _const_spec(cb.shape)],
        out_specs=[spec for spec, _ in outs],
        out_shape=[shape for _, shape in outs],
        scratch_shapes=[pltpu.VMEM(((cw.shape[0] - 1) * SUBLANES, conv_dim), F32)],
        compiler_params=_params("arbitrary"), name="ssm_in_proj",
    )(x, g.reshape(1, d), w, wdt, cw, cb)


GROUPS_PER_CHUNK = SSM_CHUNK // SUBLANES
SSD_CHUNKS_PER_STEP = 4


def _chunk_time(r):
    return r // SUBLANES + GROUPS_PER_CHUNK * (r % SUBLANES)


def _interleave_rows(v):
    l = SSM_CHUNK
    r = lax.broadcasted_iota(jnp.int32, (l, l), 0)
    c = lax.broadcasted_iota(jnp.int32, (l, l), 1)
    perm = jnp.where(c == _chunk_time(r), 1.0, 0.0).astype(BF16)
    out = [_dot(perm, v[b:b + l, :]).astype(BF16) for b in range(0, v.shape[0], l)]
    return out[0] if len(out) == 1 else jnp.concatenate(out, axis=0)


def _interleaved_conv(u, prev, cw, cb):
    l = u.shape[0]
    k_w = cw.shape[0]
    tail = prev.shape[0]
    sub8 = lax.broadcasted_iota(jnp.int32, (SUBLANES, u.shape[1]), 0)
    out = cw[k_w - 1:k_w, :] * u + cb
    for s in range(1, k_w):
        fixed = []
        for i in range(s):
            r0 = l - (s - i) * SUBLANES
            p0 = tail - (s - i) * SUBLANES
            fixed.append(jnp.where(sub8 == 0, pltpu.roll(prev[p0:p0 + SUBLANES, :], 1, 0),
                                   pltpu.roll(u[r0:r0 + SUBLANES, :], 1, 0)))
        delayed = jnp.concatenate(fixed + [u[0:l - s * SUBLANES, :]], axis=0)
        out = out + cw[k_w - 1 - s:k_w - s, :] * delayed
    return out


def _ssd_kernel(zg_ref, xs_ref, b_ref, c_ref, dt_ref, dtb_ref, alog_ref, dskip_ref, ng_ref,
                o_ref, state, yt_scr):
    l = SSM_CHUNK
    d_inner = xs_ref.shape[1]
    n_heads = d_inner // SSM_HEAD_DIM
    heads_per_group = n_heads // SSM_GROUPS
    gw = d_inner // SSM_GROUPS

    @pl.when(pl.program_id(1) == 0)
    def _():
        state[...] = jnp.zeros(state.shape, F32)

    for ci in range(xs_ref.shape[0]):
        rows = slice(ci * l, (ci + 1) * l)
        xs_t = xs_ref[ci]
        bm = b_ref[rows, :]
        cm = c_ref[rows, :]

        pre = dt_ref[rows, :] + dtb_ref[...]
        dtv = jnp.maximum(pre, 0.0) + jnp.log(1.0 + jnp.exp(-jnp.abs(pre)))
        a = dtv * (-jnp.exp(alog_ref[...]))

        t_row = _chunk_time(lax.broadcasted_iota(jnp.int32, (l, l), 0))
        t_col = _chunk_time(lax.broadcasted_iota(jnp.int32, (l, l), 1))
        tri = jnp.where(t_row >= t_col, 1.0, 0.0).astype(BF16)
        a_hi, a_mid, a_lo = _split3(a)
        cs = (_dot(tri, a_hi) + _dot(tri, a_mid) + _dot(tri, a_lo)) * LOG2E
        cml = cs - jnp.log(dtv) * LOG2E

        cs_t = cs.T
        dt_t = dtv.T
        tot = cs_t[:, l - 1:l]
        w_t = dt_t * jnp.exp2(tot - cs_t)
        g_t = jnp.broadcast_to(jnp.exp2(tot), (LANES, l))
        e_t = jnp.exp2(cs_t)
        later = t_col >= t_row

        for g in range(SSM_GROUPS):
            b_g = bm[:, g * SSM_STATE:(g + 1) * SSM_STATE]
            c_g = cm[:, g * SSM_STATE:(g + 1) * SSM_STATE]
            cbt = _dot_nt(b_g, c_g)
            yoff = _dot_nt(state[g].astype(BF16), c_g)
            for hl in range(heads_per_group):
                h = g * heads_per_group + hl
                ch = slice(h * SSM_HEAD_DIM, (h + 1) * SSM_HEAD_DIM)
                gh = slice(hl * SSM_HEAD_DIM, (hl + 1) * SSM_HEAD_DIM)
                x_h = xs_t[ch, :]
                colv = jnp.broadcast_to(cml[:, h:h + 1], (l, l))
                rowv = jnp.broadcast_to(cs_t[h:h + 1, :], (l, l))
                mt = (cbt * jnp.exp2(jnp.where(later, rowv - colv, NEG))).astype(BF16)
                y_h = _dot(x_h.astype(BF16), mt) + yoff[gh, :] * e_t[h:h + 1, :] + x_h * dskip_ref[h]
                yt_scr[ci, ch, :] = y_h
                d_s = _dot((x_h * w_t[h:h + 1, :]).astype(BF16), b_g)
                state[g, gh, :] = g_t[h:h + 1, :] * state[g, gh, :] + d_s

        r_i = lax.broadcasted_iota(jnp.int32, (l, l), 0)
        c_i = lax.broadcasted_iota(jnp.int32, (l, l), 1)
        restore = jnp.where(r_i == _chunk_time(c_i), 1.0, 0.0).astype(BF16)
        for g in range(SSM_GROUPS):
            sl = slice(g * gw, (g + 1) * gw)
            yt = yt_scr[ci, sl, :] * zg_ref[ci, sl, :]
            ms = jnp.mean(yt * yt, axis=0, keepdims=True)
            yn = (yt * lax.rsqrt(ms + EPS) * ng_ref[sl, :]).astype(BF16)
            o_ref[rows, sl] = _dot_nt(restore, yn).astype(o_ref.dtype)


def _ssd(zg, xs, bm, cm, dt, dtb, alog, dskip, ng, batch, seq):
    d_inner = xs.shape[1]
    t, gn = bm.shape
    n = SSD_CHUNKS_PER_STEP
    l = SSM_CHUNK
    ns = seq // (n * l)
    row = lambda b, c: (b * ns + c, 0)
    return pl.pallas_call(
        _ssd_kernel, grid=(batch, ns),
        in_specs=[pl.BlockSpec((n, d_inner, l), lambda b, c: (b * ns + c, 0, 0)),
                  pl.BlockSpec((n, d_inner, l), lambda b, c: (b * ns + c, 0, 0)),
                  pl.BlockSpec((n * l, gn), row), pl.BlockSpec((n * l, gn), row),
                  pl.BlockSpec((n * l, LANES), row),
                  _const_spec(dtb.shape), _const_spec(alog.shape),
                  pl.BlockSpec(memory_space=pltpu.SMEM), _const_spec(ng.shape)],
        out_specs=pl.BlockSpec((n * l, d_inner), row),
        out_shape=jax.ShapeDtypeStruct((t, d_inner), BF16),
        scratch_shapes=[pltpu.VMEM((SSM_GROUPS, d_inner // SSM_GROUPS, SSM_STATE), F32),
                        pltpu.VMEM((n, d_inner, l), F32)],
        compiler_params=_params("arbitrary", "arbitrary"), name="ssd",
    )(zg, xs, bm, cm, dt, dtb, alog, dskip, ng)


def _ffn_kernel(tiles_per_seq, y_ref, wmix_ref, x_ref, g_ref, wup_ref, cw_ref, cb_ref, wdn_ref,
                o_ref, ctail):
    tm = x_ref.shape[0]
    f = wdn_ref.shape[0]

    @pl.when(pl.program_id(0) % tiles_per_seq == 0)
    def _():
        ctail[...] = jnp.zeros(ctail.shape, F32)

    x = x_ref[...] + _dot(y_ref[...], wmix_ref[...])
    xn = _rms(x, g_ref[...]).astype(BF16)
    u = _dot(xn, wup_ref[...])
    prev = ctail[...]
    sub8 = lax.broadcasted_iota(jnp.int32, (SUBLANES, u.shape[1]), 0)
    k_w = cw_ref.shape[0]
    conv = cw_ref[k_w - 1:k_w, :] * u + cb_ref[...]
    for s in range(1, k_w):
        rolled = pltpu.roll(u, s, 0)
        head = jnp.where(sub8 < s, pltpu.roll(prev, s, 0), rolled[:SUBLANES, :])
        delayed = jnp.concatenate([head, rolled[SUBLANES:, :]], axis=0)
        conv = conv + cw_ref[k_w - 1 - s:k_w - s, :] * delayed
    ctail[...] = u[tm - SUBLANES:tm, :]
    hid = (_silu(conv[:, :f]) * conv[:, f:]).astype(BF16)
    o_ref[...] = x + _dot(hid, wdn_ref[...])


def _layer_spec(stack, layer):
    nd = stack.ndim - 1
    return pl.BlockSpec((None,) + stack.shape[1:], lambda *_: (layer,) + (0,) * nd,
                        pipeline_mode=pl.Buffered(1))


def _ffn(y, wmix, x, g, wup, cw, cb, wdn, layer, seq, name):
    t, d = x.shape
    tm = FFN_TILE
    return pl.pallas_call(
        functools.partial(_ffn_kernel, seq // tm), grid=(t // tm,),
        in_specs=[pl.BlockSpec((tm, y.shape[1]), lambda i: (i, 0)), _const_spec(wmix.shape),
                  pl.BlockSpec((tm, d), lambda i: (i, 0)), _const_spec((1, d)),
                  _layer_spec(wup, layer), _const_spec(cw.shape), _const_spec((1, cb.shape[0])),
                  _layer_spec(wdn, layer)],
        out_specs=pl.BlockSpec((tm, d), lambda i: (i, 0)),
        out_shape=jax.ShapeDtypeStruct((t, d), F32),
        scratch_shapes=[pltpu.VMEM((SUBLANES, wup.shape[2]), F32)],
        compiler_params=_params("arbitrary"), name=name,
    )(y, wmix, x, g.reshape(1, d), wup, cw, cb.reshape(1, -1), wdn)


def _pair_norm(blk, g2):
    lane = lax.broadcasted_iota(jnp.int32, blk.shape, 1)
    lo = lane < ATT_HEAD_DIM
    sq = blk * blk
    s_lo = jnp.sum(jnp.where(lo, sq, 0.0), axis=-1, keepdims=True)
    s_hi = jnp.sum(jnp.where(lo, 0.0, sq), axis=-1, keepdims=True)
    inv = 1.0 / ATT_HEAD_DIM
    rs = jnp.where(lo, lax.rsqrt(s_lo * inv + EPS), lax.rsqrt(s_hi * inv + EPS))
    return blk * rs * g2


def _qkv_kernel(x_ref, gkv_ref, gq_ref, wkv_ref, wq_ref, kg_ref, qg_ref, k_ref, v_ref, q_ref):
    n_heads = k_ref.shape[0]
    tm = x_ref.shape[0]
    t = q_ref.shape[2] // 2
    x = x_ref[...]
    xr = x * lax.rsqrt(jnp.mean(x * x, axis=-1, keepdims=True) + EPS)
    kv = _dot((xr * gkv_ref[...]).astype(BF16), wkv_ref[...])
    q = _dot((xr * gq_ref[...]).astype(BF16), wq_ref[...])
    kd = n_heads * LANES
    lane = lax.broadcasted_iota(jnp.int32, (tm, LANES), 1)
    lo = lane < ATT_HEAD_DIM
    scale = ATT_HEAD_DIM ** -0.5 * LOG2E
    for h in range(n_heads):
        sl = slice(h * LANES, (h + 1) * LANES)
        k_ref[h] = _pair_norm(kv[:, sl], kg_ref[...]).astype(k_ref.dtype)
        v_ref[h] = kv[:, kd + h * LANES:kd + (h + 1) * LANES].astype(v_ref.dtype)
        qn = _pair_norm(q[:, sl], qg_ref[...]) * scale
        q0 = jnp.where(lo, qn, 0.0).astype(q_ref.dtype)
        q1 = jnp.where(lo, 0.0, qn).astype(q_ref.dtype)
        for qt in range(tm // t):
            q_ref[h, qt, 0:t, :] = q0[qt * t:(qt + 1) * t, :]
            q_ref[h, qt, t:2 * t, :] = q1[qt * t:(qt + 1) * t, :]


def _qkv_proj(x, gkv, gq, wkv, wq, kg2, qg2, n_heads, t):
    tt, d = x.shape
    tm = QKV_TILE
    hm = jax.ShapeDtypeStruct((n_heads, tt, LANES), BF16)
    hspec = pl.BlockSpec((n_heads, tm, LANES), lambda i: (0, i, 0))
    return pl.pallas_call(
        _qkv_kernel, grid=(tt // tm,),
        in_specs=[pl.BlockSpec((tm, d), lambda i: (i, 0)), _const_spec((1, d)), _const_spec((1, d)),
                  _const_spec(wkv.shape), _const_spec(wq.shape), _const_spec((1, LANES)),
                  _const_spec((1, LANES))],
        out_specs=[hspec, hspec, pl.BlockSpec((n_heads, tm // t, 2 * t, LANES), lambda i: (0, i, 0, 0))],
        out_shape=[hm, hm, jax.ShapeDtypeStruct((n_heads, tt // t, 2 * t, LANES), BF16)],
        compiler_params=_params("arbitrary"), name="qkv_proj",
    )(x, gkv.reshape(1, d), gq.reshape(1, d), wkv, wq, kg2, qg2)


def _bucket_thresholds():
    n = np.arange(1, 4 * MAX_DISTANCE, dtype=np.int64)
    max_exact = NUM_BUCKETS // 2
    nf = n.astype(np.float32)
    large = max_exact + (np.log(nf / np.float32(max_exact)) / np.float32(math.log(MAX_DISTANCE / max_exact))
                         * np.float32(NUM_BUCKETS - max_exact)).astype(np.int32)
    large = np.minimum(large, NUM_BUCKETS - 1)
    bucket = np.where(n < max_exact, n, large)
    return [int(n[np.argmax(bucket >= b)]) for b in range(max_exact + 1, NUM_BUCKETS)]


def _bias_kernel(thresholds, rb_ref, o_ref):
    t = o_ref.shape[1]
    h = pl.program_id(0)
    max_exact = NUM_BUCKETS // 2
    row = lax.broadcasted_iota(jnp.int32, (t, t), 0)
    col = lax.broadcasted_iota(jnp.int32, (t, t), 1)
    far = rb_ref[NUM_BUCKETS - 1, h]
    for tiles_left in (1, 0):
        d = row - col + tiles_left * t
        n = jnp.maximum(d, 0)
        bucket = jnp.minimum(n, max_exact)
        for thr in thresholds:
            bucket = bucket + jnp.where(n >= thr, 1, 0)
        val = jnp.zeros((t, t), F32)
        for b in range(NUM_BUCKETS):
            val = jnp.where(bucket == b, (rb_ref[b, h] - far) * LOG2E, val)
        if tiles_left == 0:
            val = jnp.where(d >= 0, val, NEG)
        o_ref[0, :, (1 - tiles_left) * t:(2 - tiles_left) * t] = val


def _bias_tiles(rel_bias, n_heads, t):
    return pl.pallas_call(
        functools.partial(_bias_kernel, _bucket_thresholds()), grid=(n_heads,),
        in_specs=[pl.BlockSpec(memory_space=pltpu.SMEM)],
        out_specs=pl.BlockSpec((1, t, 2 * t), lambda h: (h, 0, 0)),
        out_shape=jax.ShapeDtypeStruct((n_heads, t, 2 * t), F32),
        compiler_params=_params("arbitrary"), name="rel_bias_tiles",
    )(rel_bias)


def _attn_kernel(lam_init, q_scr, lv_ref, sg_ref, k_ref, v_ref, bias_ref, o_ref, m_scr, acc_scr):
    t = o_ref.shape[0]
    n_heads = k_ref.shape[0]
    qi = pl.program_id(1)

    lv = lv_ref[...]
    lam = (jnp.exp(jnp.sum(lv[0:1] * lv[1:2], axis=-1, keepdims=True))
           - jnp.exp(jnp.sum(lv[2:3] * lv[3:4], axis=-1, keepdims=True)) + lam_init)

    def step(j, tiles, bias_col, first=False):
        w = tiles * t
        start = pl.multiple_of(j * t, t)
        ones = jnp.ones((w, LANES), BF16)
        for h in range(n_heads):
            kt = k_ref[h, pl.ds(start, w), :]
            vx = jnp.concatenate([v_ref[h, pl.ds(start, w), :], ones], axis=1)
            for r0 in range(0, 2 * t, ATT_ROWS):
                rows = slice(r0, r0 + ATT_ROWS)
                s = _dot_nt(q_scr[h, rows, :], kt)
                if bias_col is not None:
                    b0 = r0 % t
                    s = s + bias_ref[h, b0:b0 + ATT_ROWS, bias_col:bias_col + w]
                m_cur = jnp.max(s, axis=-1, keepdims=True)
                if first:
                    m_new = jnp.broadcast_to(m_cur, (ATT_ROWS, LANES))
                else:
                    m_prev = m_scr[h, rows, :]
                    m_new = jnp.maximum(m_prev, m_cur)
                p = jnp.exp2(s - jnp.concatenate([m_new] * (w // LANES), axis=1)).astype(BF16)
                pv = _dot(p, vx)
                if not first:
                    alpha = jnp.exp2(m_prev - m_new)
                    pv = jnp.concatenate([alpha, alpha], axis=1) * acc_scr[h, rows, :] + pv
                acc_scr[h, rows, :] = pv
                m_scr[h, rows, :] = m_new

    @pl.when(qi == 0)
    def _():
        step(0, 1, t, first=True)

    @pl.when(qi >= 1)
    def _():
        step(qi - 1, 2, 0, first=True)

    n_far = jnp.maximum(qi - 1, 0)

    def far_step(jj, c):
        step(2 * jj, 2, None)
        return c

    lax.fori_loop(0, n_far // 2, far_step, 0)

    @pl.when(n_far % 2 == 1)
    def _():
        step(n_far - 1, 1, None)

    for h in range(n_heads):
        acc = acc_scr[h]
        o = acc[:, :LANES] / acc[:, LANES:]
        d = o[:t] - lam * o[t:]
        d = _rms(d, sg_ref[...]) * (1.0 - lam_init)
        o_ref[:, h * LANES:(h + 1) * LANES] = d.astype(o_ref.dtype)


def _attention(qs, lam_vecs, sg, k, v, bias, lam_init, batch, seq):
    n_heads, tt, _ = k.shape
    t = ATT_TILE
    nq = seq // t
    kd = n_heads * LANES
    kv_spec = pl.BlockSpec((n_heads, seq, LANES), lambda b, i: (0, b, 0))
    return pl.pallas_call(
        functools.partial(_attn_kernel, lam_init), grid=(batch, nq),
        in_specs=[pl.BlockSpec((n_heads, None, 2 * t, LANES), lambda b, i: (0, b * nq + i, 0, 0)),
                  _const_spec(lam_vecs.shape), _const_spec((1, LANES)), kv_spec, kv_spec,
                  _const_spec(bias.shape)],
        out_specs=pl.BlockSpec((t, kd), lambda b, i: (b * nq + i, 0)),
        out_shape=jax.ShapeDtypeStruct((tt, kd), BF16),
        scratch_shapes=[pltpu.VMEM((n_heads, 2 * t, LANES), F32),
                        pltpu.VMEM((n_heads, 2 * t, 2 * LANES), F32)],
        compiler_params=_params("arbitrary", "arbitrary"), name="diff_attention",
    )(qs, lam_vecs, sg.reshape(1, LANES), k, v, bias)


def kernel(x, ssm_ln_g, ssm_in_w, ssm_conv_w, ssm_conv_b, ssm_dt_bias, ssm_a_log, ssm_d, ssm_norm_g, ssm_out_w, kv_ln_g, kv_w, k_norm_g, rel_bias, attn_ln_g, q_w, q_norm_g, lam_vecs, subln_g, attn_out_w, ffn_ln_g, ffn_up_w, ffn_conv_w, ffn_conv_b, ffn_down_w):
    batch, seq, d = x.shape
    t = batch * seq
    n_a = ssm_in_w.shape[0]
    depth = ffn_up_w.shape[0]
    h = x.reshape(t, d)
    k_sh = v_sh = bias = None
    n_att_heads = q_w.shape[2] // LANES
    ffn_up = ffn_up_w.astype(BF16)
    ffn_down = ffn_down_w.astype(BF16)

    for layer in range(depth):
        if layer < n_a:
            i = layer
            d_inner = ssm_out_w.shape[1]
            n_ssm_heads = ssm_dt_bias.shape[1]
            conv_dim = ssm_conv_w.shape[2]
            w_dt = jnp.pad(ssm_in_w[i][:, d_inner + conv_dim:].astype(BF16),
                           ((0, 0), (0, LANES - n_ssm_heads)))
            zg, xs, bm, cm, dt = _in_proj(h, ssm_ln_g[i], ssm_in_w[i].astype(BF16), w_dt, ssm_conv_w[i],
                                          ssm_conv_b[i].reshape(1, -1), d_inner, seq)
            pad_h = ((0, 0), (0, LANES - n_ssm_heads))
            mix = _ssd(zg, xs, bm, cm, dt, jnp.pad(ssm_dt_bias[i].reshape(1, -1), pad_h),
                       jnp.pad(ssm_a_log[i].reshape(1, -1), pad_h),
                       ssm_d[i], jnp.broadcast_to(ssm_norm_g[i][:, None], (d_inner, LANES)), batch, seq)
            w_mix = ssm_out_w[i].astype(BF16)
        else:
            j = layer - n_a
            if j == 0:
                bias = _bias_tiles(rel_bias, n_att_heads, ATT_TILE)
            k_new, v_new, qs = _qkv_proj(h, kv_ln_g, attn_ln_g[j], kv_w.astype(BF16), q_w[j].astype(BF16),
                                         jnp.tile(k_norm_g, 2).reshape(1, LANES),
                                         jnp.tile(q_norm_g[j], 2).reshape(1, LANES), n_att_heads, ATT_TILE)
            if j == 0:
                k_sh, v_sh = k_new, v_new
            lam_init = 0.8 - 0.6 * math.exp(-0.3 * layer)
            mix = _attention(qs, lam_vecs[j], subln_g[j], k_sh, v_sh, bias, lam_init, batch, seq)
            w_mix = attn_out_w[j].astype(BF16)
        h = _ffn(mix, w_mix, h, ffn_ln_g[layer], ffn_up, ffn_conv_w[layer], ffn_conv_b[layer],
                 ffn_down, layer, seq, f"conv_ffn_{layer}")
    return h.reshape(batch, seq, d)
```

```python
import functools
import math

import numpy as np
import jax
import jax.numpy as jnp
from jax import lax
from jax.experimental import pallas as pl
from jax.experimental.pallas import tpu as pltpu

F32 = jnp.float32
BF16 = jnp.bfloat16

EPS = 1e-6
NEG = -1e30
LOG2E = math.log2(math.e)

SSM_GROUPS = 4
SSM_STATE = 128
SSM_HEAD_DIM = 64
SSM_CHUNK = 128
ATT_HEAD_DIM = 64
NUM_BUCKETS = 32
MAX_DISTANCE = 128
LANES = 128
SUBLANES = 8
VMEM_LIMIT = 56 * 1024 * 1024

ROW_TILE = 512
QKV_TILE = 1024
FFN_TILE = 512
ATT_TILE = 256
ATT_ROWS = 128


def _params(*sem):
    return pltpu.CompilerParams(dimension_semantics=sem, vmem_limit_bytes=VMEM_LIMIT)


def _const_spec(shape):
    nd = len(shape)
    return pl.BlockSpec(shape, lambda *_: (0,) * nd, pipeline_mode=pl.Buffered(1))


def _rms(x, g):
    ms = jnp.mean(x * x, axis=-1, keepdims=True)
    return x * lax.rsqrt(ms + EPS) * g


def _silu(x):
    h = 0.5 * x
    return h * jnp.tanh(h) + h


def _split3(v):
    hi = v.astype(BF16)
    r1 = v - hi.astype(F32)
    mid = r1.astype(BF16)
    lo = (r1 - mid.astype(F32)).astype(BF16)
    return hi, mid, lo


def _dot(a, b):
    return jnp.dot(a, b, preferred_element_type=F32)


def _dot_nt(a, b):
    return lax.dot_general(a, b, (((1,), (1,)), ((), ())), preferred_element_type=F32)


def _in_proj_kernel(tiles_per_seq, x_ref, g_ref, w_ref, wdt_ref, cw_ref, cb_ref,
                    zg_ref, xs_ref, b_ref, c_ref, dt_ref, ctail):
    tm = x_ref.shape[0]
    l = SSM_CHUNK
    d_inner = xs_ref.shape[1]
    conv_dim = cw_ref.shape[1]
    gn = b_ref.shape[1]
    tail = ctail.shape[0]

    @pl.when(pl.program_id(0) % tiles_per_seq == 0)
    def _():
        ctail[...] = jnp.zeros(ctail.shape, F32)

    xn = _interleave_rows(_rms(x_ref[...], g_ref[...]).astype(BF16))
    xbc = _dot(xn, w_ref[:, d_inner:d_inner + conv_dim])
    zg = _silu(_dot(xn, w_ref[:, :d_inner]))
    for c0 in range(0, tm, l):
        zg_ref[c0 // l] = zg[c0:c0 + l, :].T
    dt_ref[...] = _dot(xn, wdt_ref[...])
    cw_half = 0.5 * cw_ref[...]
    cb_half = 0.5 * cb_ref[...]
    prev = ctail[...]
    for c0 in range(0, tm, l):
        u = xbc[c0:c0 + l, :]
        half = _interleaved_conv(u, prev, cw_half, cb_half)
        act = half * jnp.tanh(half) + half
        xs_ref[c0 // l] = act[:, :d_inner].T
        b_ref[c0:c0 + l, :] = act[:, d_inner:d_inner + gn].astype(b_ref.dtype)
        c_ref[c0:c0 + l, :] = act[:, d_inner + gn:].astype(c_ref.dtype)
        prev = u[l - tail:l, :]
    ctail[...] = prev


def _in_proj(x, g, w, wdt, cw, cb, d_inner, seq):
    t, d = x.shape
    tm = ROW_TILE
    l = SSM_CHUNK
    conv_dim = cw.shape[1]
    gn = (conv_dim - d_inner) // 2
    rows = lambda n, dt: (pl.BlockSpec((tm, n), lambda i: (i, 0)), jax.ShapeDtypeStruct((t, n), dt))
    chunked = (pl.BlockSpec((tm // l, d_inner, l), lambda i: (i, 0, 0)),
               jax.ShapeDtypeStruct((t // l, d_inner, l), F32))
    outs = (chunked, chunked, rows(gn, BF16), rows(gn, BF16), rows(LANES, F32))
    return pl.pallas_call(
        functools.partial(_in_proj_kernel, seq // tm), grid=(t // tm,),
        in_specs=[pl.BlockSpec((tm, d), lambda i: (i, 0)), _const_spec((1, d)),
                  _const_spec(w.shape), _const_spec(wdt.shape), _const_spec(cw.shape),
                  _const_spec(cb.shape)],
        out_specs=[spec for spec, _ in outs],
        out_shape=[shape for _, shape in outs],
        scratch_shapes=[pltpu.VMEM(((cw.shape[0] - 1) * SUBLANES, conv_dim), F32)],
        compiler_params=_params("arbitrary"), name="ssm_in_proj",
    )(x, g.reshape(1, d), w, wdt, cw, cb)


GROUPS_PER_CHUNK = SSM_CHUNK // SUBLANES
SSD_CHUNKS_PER_STEP = 4


def _chunk_time(r):
    return r // SUBLANES + GROUPS_PER_CHUNK * (r % SUBLANES)


def _interleave_rows(v):
    l = SSM_CHUNK
    r = lax.broadcasted_iota(jnp.int32, (l, l), 0)
    c = lax.broadcasted_iota(jnp.int32, (l, l), 1)
    perm = jnp.where(c == _chunk_time(r), 1.0, 0.0).astype(BF16)
    out = [_dot(perm, v[b:b + l, :]).astype(BF16) for b in range(0, v.shape[0], l)]
    return out[0] if len(out) == 1 else jnp.concatenate(out, axis=0)


def _interleaved_conv(u, prev, cw, cb):
    l = u.shape[0]
    k_w = cw.shape[0]
    tail = prev.shape[0]
    sub8 = lax.broadcasted_iota(jnp.int32, (SUBLANES, u.shape[1]), 0)
    out = cw[k_w - 1:k_w, :] * u + cb
    for s in range(1, k_w):
        fixed = []
        for i in range(s):
            r0 = l - (s - i) * SUBLANES
            p0 = tail - (s - i) * SUBLANES
            fixed.append(jnp.where(sub8 == 0, pltpu.roll(prev[p0:p0 + SUBLANES, :], 1, 0),
                                   pltpu.roll(u[r0:r0 + SUBLANES, :], 1, 0)))
        delayed = jnp.concatenate(fixed + [u[0:l - s * SUBLANES, :]], axis=0)
        out = out + cw[k_w - 1 - s:k_w - s, :] * delayed
    return out


def _ssd_kernel(zg_ref, xs_ref, b_ref, c_ref, dt_ref, dtb_ref, alog_ref, dskip_ref, ng_ref,
                o_ref, state, yt_scr):
    l = SSM_CHUNK
    d_inner = xs_ref.shape[1]
    n_heads = d_inner // SSM_HEAD_DIM
    heads_per_group = n_heads // SSM_GROUPS
    gw = d_inner // SSM_GROUPS

    @pl.when(pl.program_id(1) == 0)
    def _():
        state[...] = jnp.zeros(state.shape, F32)

    for ci in range(xs_ref.shape[0]):
        rows = slice(ci * l, (ci + 1) * l)
        xs_t = xs_ref[ci]
        bm = b_ref[rows, :]
        cm = c_ref[rows, :]

        pre = dt_ref[rows, :] + dtb_ref[...]
        dtv = jnp.maximum(pre, 0.0) + jnp.log(1.0 + jnp.exp(-jnp.abs(pre)))
        a = dtv * (-jnp.exp(alog_ref[...]))

        t_row = _chunk_time(lax.broadcasted_iota(jnp.int32, (l, l), 0))
        t_col = _chunk_time(lax.broadcasted_iota(jnp.int32, (l, l), 1))
        tri = jnp.where(t_row >= t_col, 1.0, 0.0).astype(BF16)
        a_hi, a_mid, a_lo = _split3(a)
        cs = (_dot(tri, a_hi) + _dot(tri, a_mid) + _dot(tri, a_lo)) * LOG2E
        cml = cs - jnp.log(dtv) * LOG2E

        cs_t = cs.T
        dt_t = dtv.T
        tot = cs_t[:, l - 1:l]
        w_t = dt_t * jnp.exp2(tot - cs_t)
        g_t = jnp.broadcast_to(jnp.exp2(tot), (LANES, l))
        e_t = jnp.exp2(cs_t)
        later = t_col >= t_row

        for g in range(SSM_GROUPS):
            b_g = bm[:, g * SSM_STATE:(g + 1) * SSM_STATE]
            c_g = cm[:, g * SSM_STATE:(g + 1) * SSM_STATE]
            cbt = _dot_nt(b_g, c_g)
            yoff = _dot_nt(state[g].astype(BF16), c_g)
            for hl in range(heads_per_group):
                h = g * heads_per_group + hl
                ch = slice(h * SSM_HEAD_DIM, (h + 1) * SSM_HEAD_DIM)
                gh = slice(hl * SSM_HEAD_DIM, (hl + 1) * SSM_HEAD_DIM)
                x_h = xs_t[ch, :]
                colv = jnp.broadcast_to(cml[:, h:h + 1], (l, l))
                rowv = jnp.broadcast_to(cs_t[h:h + 1, :], (l, l))
                mt = (cbt * jnp.exp2(jnp.where(later, rowv - colv, NEG))).astype(BF16)
                y_h = _dot(x_h.astype(BF16), mt) + yoff[gh, :] * e_t[h:h + 1, :] + x_h * dskip_ref[h]
                yt_scr[ci, ch, :] = y_h
                d_s = _dot((x_h * w_t[h:h + 1, :]).astype(BF16), b_g)
                state[g, gh, :] = g_t[h:h + 1, :] * state[g, gh, :] + d_s

        r_i = lax.broadcasted_iota(jnp.int32, (l, l), 0)
        c_i = lax.broadcasted_iota(jnp.int32, (l, l), 1)
        restore = jnp.where(r_i == _chunk_time(c_i), 1.0, 0.0).astype(BF16)
        for g in range(SSM_GROUPS):
            sl = slice(g * gw, (g + 1) * gw)
            yt = yt_scr[ci, sl, :] * zg_ref[ci, sl, :]
            ms = jnp.mean(yt * yt, axis=0, keepdims=True)
            yn = (yt * lax.rsqrt(ms + EPS) * ng_ref[sl, :]).astype(BF16)
            o_ref[rows, sl] = _dot_nt(restore, yn).astype(o_ref.dtype)


def _ssd(zg, xs, bm, cm, dt, dtb, alog, dskip, ng, batch, seq):
    d_inner = xs.shape[1]
    t, gn = bm.shape
    n = SSD_CHUNKS_PER_STEP
    l = SSM_CHUNK
    ns = seq // (n * l)
    row = lambda b, c: (b * ns + c, 0)
    return pl.pallas_call(
        _ssd_kernel, grid=(batch, ns),
        in_specs=[pl.BlockSpec((n, d_inner, l), lambda b, c: (b * ns + c, 0, 0)),
                  pl.BlockSpec((n, d_inner, l), lambda b, c: (b * ns + c, 0, 0)),
                  pl.BlockSpec((n * l, gn), row), pl.BlockSpec((n * l, gn), row),
                  pl.BlockSpec((n * l, LANES), row),
                  _const_spec(dtb.shape), _const_spec(alog.shape),
                  pl.BlockSpec(memory_space=pltpu.SMEM), _const_spec(ng.shape)],
        out_specs=pl.BlockSpec((n * l, d_inner), row),
        out_shape=jax.ShapeDtypeStruct((t, d_inner), BF16),
        scratch_shapes=[pltpu.VMEM((SSM_GROUPS, d_inner // SSM_GROUPS, SSM_STATE), F32),
                        pltpu.VMEM((n, d_inner, l), F32)],
        compiler_params=_params("arbitrary", "arbitrary"), name="ssd",
    )(zg, xs, bm, cm, dt, dtb, alog, dskip, ng)


def _ffn_kernel(tiles_per_seq, y_ref, wmix_ref, x_ref, g_ref, wup_ref, cw_ref, cb_ref, wdn_ref,
                o_ref, ctail):
    tm = x_ref.shape[0]
    f = wdn_ref.shape[0]

    @pl.when(pl.program_id(0) % tiles_per_seq == 0)
    def _():
        ctail[...] = jnp.zeros(ctail.shape, F32)

    x = x_ref[...] + _dot(y_ref[...], wmix_ref[...])
    xn = _rms(x, g_ref[...]).astype(BF16)
    u = _dot(xn, wup_ref[...])
    prev = ctail[...]
    sub8 = lax.broadcasted_iota(jnp.int32, (SUBLANES, u.shape[1]), 0)
    k_w = cw_ref.shape[0]
    conv = cw_ref[k_w - 1:k_w, :] * u + cb_ref[...]
    for s in range(1, k_w):
        rolled = pltpu.roll(u, s, 0)
        head = jnp.where(sub8 < s, pltpu.roll(prev, s, 0), rolled[:SUBLANES, :])
        delayed = jnp.concatenate([head, rolled[SUBLANES:, :]], axis=0)
        conv = conv + cw_ref[k_w - 1 - s:k_w - s, :] * delayed
    ctail[...] = u[tm - SUBLANES:tm, :]
    hid = (_silu(conv[:, :f]) * conv[:, f:]).astype(BF16)
    o_ref[...] = x + _dot(hid, wdn_ref[...])


def _layer_spec(stack, layer):
    nd = stack.ndim - 1
    return pl.BlockSpec((None,) + stack.shape[1:], lambda *_: (layer,) + (0,) * nd,
                        pipeline_mode=pl.Buffered(1))


def _ffn(y, wmix, x, g, wup, cw, cb, wdn, layer, seq, name):
    t, d = x.shape
    tm = FFN_TILE
    return pl.pallas_call(
        functools.partial(_ffn_kernel, seq // tm), grid=(t // tm,),
        in_specs=[pl.BlockSpec((tm, y.shape[1]), lambda i: (i, 0)), _const_spec(wmix.shape),
                  pl.BlockSpec((tm, d), lambda i: (i, 0)), _const_spec((1, d)),
                  _layer_spec(wup, layer), _const_spec(cw.shape), _const_spec((1, cb.shape[0])),
                  _layer_spec(wdn, layer)],
        out_specs=pl.BlockSpec((tm, d), lambda i: (i, 0)),
        out_shape=jax.ShapeDtypeStruct((t, d), F32),
        scratch_shapes=[pltpu.VMEM((SUBLANES, wup.shape[2]), F32)],
        compiler_params=_params("arbitrary"), name=name,
    )(y, wmix, x, g.reshape(1, d), wup, cw, cb.reshape(1, -1), wdn)


def _pair_norm(blk, g2):
    lane = lax.broadcasted_iota(jnp.int32, blk.shape, 1)
    lo = lane < ATT_HEAD_DIM
    sq = blk * blk
    s_lo = jnp.sum(jnp.where(lo, sq, 0.0), axis=-1, keepdims=True)
    s_hi = jnp.sum(jnp.where(lo, 0.0, sq), axis=-1, keepdims=True)
    inv = 1.0 / ATT_HEAD_DIM
    rs = jnp.where(lo, lax.rsqrt(s_lo * inv + EPS), lax.rsqrt(s_hi * inv + EPS))
    return blk * rs * g2


def _qkv_kernel(x_ref, gkv_ref, gq_ref, wkv_ref, wq_ref, kg_ref, qg_ref, k_ref, v_ref, q_ref):
    n_heads = k_ref.shape[0]
    tm = x_ref.shape[0]
    t = q_ref.shape[2] // 2
    x = x_ref[...]
    xr = x * lax.rsqrt(jnp.mean(x * x, axis=-1, keepdims=True) + EPS)
    kv = _dot((xr * gkv_ref[...]).astype(BF16), wkv_ref[...])
    q = _dot((xr * gq_ref[...]).astype(BF16), wq_ref[...])
    kd = n_heads * LANES
    lane = lax.broadcasted_iota(jnp.int32, (tm, LANES), 1)
    lo = lane < ATT_HEAD_DIM
    scale = ATT_HEAD_DIM ** -0.5 * LOG2E
    for h in range(n_heads):
        sl = slice(h * LANES, (h + 1) * LANES)
        k_ref[h] = _pair_norm(kv[:, sl], kg_ref[...]).astype(k_ref.dtype)
        v_ref[h] = kv[:, kd + h * LANES:kd + (h + 1) * LANES].astype(v_ref.dtype)
        qn = _pair_norm(q[:, sl], qg_ref[...]) * scale
        q0 = jnp.where(lo, qn, 0.0).astype(q_ref.dtype)
        q1 = jnp.where(lo, 0.0, qn).astype(q_ref.dtype)
        for qt in range(tm // t):
            q_ref[h, qt, 0:t, :] = q0[qt * t:(qt + 1) * t, :]
            q_ref[h, qt, t:2 * t, :] = q1[qt * t:(qt + 1) * t, :]


def _qkv_proj(x, gkv, gq, wkv, wq, kg2, qg2, n_heads, t):
    tt, d = x.shape
    tm = QKV_TILE
    hm = jax.ShapeDtypeStruct((n_heads, tt, LANES), BF16)
    hspec = pl.BlockSpec((n_heads, tm, LANES), lambda i: (0, i, 0))
    return pl.pallas_call(
        _qkv_kernel, grid=(tt // tm,),
        in_specs=[pl.BlockSpec((tm, d), lambda i: (i, 0)), _const_spec((1, d)), _const_spec((1, d)),
                  _const_spec(wkv.shape), _const_spec(wq.shape), _const_spec((1, LANES)),
                  _const_spec((1, LANES))],
        out_specs=[hspec, hspec, pl.BlockSpec((n_heads, tm // t, 2 * t, LANES), lambda i: (0, i, 0, 0))],
        out_shape=[hm, hm, jax.ShapeDtypeStruct((n_heads, tt // t, 2 * t, LANES), BF16)],
        compiler_params=_params("arbitrary"), name="qkv_proj",
    )(x, gkv.reshape(1, d), gq.reshape(1, d), wkv, wq, kg2, qg2)


def _bucket_thresholds():
    n = np.arange(1, 4 * MAX_DISTANCE, dtype=np.int64)
    max_exact = NUM_BUCKETS // 2
    nf = n.astype(np.float32)
    large = max_exact + (np.log(nf / np.float32(max_exact)) / np.float32(math.log(MAX_DISTANCE / max_exact))
                         * np.float32(NUM_BUCKETS - max_exact)).astype(np.int32)
    large = np.minimum(large, NUM_BUCKETS - 1)
    bucket = np.where(n < max_exact, n, large)
    return [int(n[np.argmax(bucket >= b)]) for b in range(max_exact + 1, NUM_BUCKETS)]


def _bias_kernel(thresholds, rb_ref, o_ref):
    t = o_ref.shape[1]
    h = pl.program_id(0)
    max_exact = NUM_BUCKETS // 2
    row = lax.broadcasted_iota(jnp.int32, (t, t), 0)
    col = lax.broadcasted_iota(jnp.int32, (t, t), 1)
    far = rb_ref[NUM_BUCKETS - 1, h]
    for tiles_left in (1, 0):
        d = row - col + tiles_left * t
        n = jnp.maximum(d, 0)
        bucket = jnp.minimum(n, max_exact)
        for thr in thresholds:
            bucket = bucket + jnp.where(n >= thr, 1, 0)
        val = jnp.zeros((t, t), F32)
        for b in range(NUM_BUCKETS):
            val = jnp.where(bucket == b, (rb_ref[b, h] - far) * LOG2E, val)
        if tiles_left == 0:
            val = jnp.where(d >= 0, val, NEG)
        o_ref[0, :, (1 - tiles_left) * t:(2 - tiles_left) * t] = val


def _bias_tiles(rel_bias, n_heads, t):
    return pl.pallas_call(
        functools.partial(_bias_kernel, _bucket_thresholds()), grid=(n_heads,),
        in_specs=[pl.BlockSpec(memory_space=pltpu.SMEM)],
        out_specs=pl.BlockSpec((1, t, 2 * t), lambda h: (h, 0, 0)),
        out_shape=jax.ShapeDtypeStruct((n_heads, t, 2 * t), F32),
        compiler_params=_params("arbitrary"), name="rel_bias_tiles",
    )(rel_bias)


def _attn_kernel(lam_init, q_scr, lv_ref, sg_ref, k_ref, v_ref, bias_ref, o_ref, m_scr, acc_scr):
    t = o_ref.shape[0]
    n_heads = k_ref.shape[0]
    qi = pl.program_id(1)

    lv = lv_ref[...]
    lam = (jnp.exp(jnp.sum(lv[0:1] * lv[1:2], axis=-1, keepdims=True))
           - jnp.exp(jnp.sum(lv[2:3] * lv[3:4], axis=-1, keepdims=True)) + lam_init)

    def step(j, tiles, bias_col, first=False):
        w = tiles * t
        start = pl.multiple_of(j * t, t)
        ones = jnp.ones((w, LANES), BF16)
        for h in range(n_heads):
            kt = k_ref[h, pl.ds(start, w), :]
            vx = jnp.concatenate([v_ref[h, pl.ds(start, w), :], ones], axis=1)
            for r0 in range(0, 2 * t, ATT_ROWS):
                rows = slice(r0, r0 + ATT_ROWS)
                s = _dot_nt(q_scr[h, rows, :], kt)
                if bias_col is not None:
                    b0 = r0 % t
                    s = s + bias_ref[h, b0:b0 + ATT_ROWS, bias_col:bias_col + w]
                m_cur = jnp.max(s, axis=-1, keepdims=True)
                if first:
                    m_new = jnp.broadcast_to(m_cur, (ATT_ROWS, LANES))
                else:
                    m_prev = m_scr[h, rows, :]
                    m_new = jnp.maximum(m_prev, m_cur)
                p = jnp.exp2(s - jnp.concatenate([m_new] * (w // LANES), axis=1)).astype(BF16)
                pv = _dot(p, vx)
                if not first:
                    alpha = jnp.exp2(m_prev - m_new)
                    pv = jnp.concatenate([alpha, alpha], axis=1) * acc_scr[h, rows, :] + pv
                acc_scr[h, rows, :] = pv
                m_scr[h, rows, :] = m_new

    @pl.when(qi == 0)
    def _():
        step(0, 1, t, first=True)

    @pl.when(qi >= 1)
    def _():
        step(qi - 1, 2, 0, first=True)

    n_far = jnp.maximum(qi - 1, 0)

    def far_step(jj, c):
        step(2 * jj, 2, None)
        return c

    lax.fori_loop(0, n_far // 2, far_step, 0)

    @pl.when(n_far % 2 == 1)
    def _():
        step(n_far - 1, 1, None)

    for h in range(n_heads):
        for r0 in range(0, t, ATT_ROWS):
            a0 = acc_scr[h, r0:r0 + ATT_ROWS, :]
            a1 = acc_scr[h, t + r0:t + r0 + ATT_ROWS, :]
            d = a0[:, :LANES] / a0[:, LANES:] - lam * (a1[:, :LANES] / a1[:, LANES:])
            d = _rms(d, sg_ref[...]) * (1.0 - lam_init)
            o_ref[r0:r0 + ATT_ROWS, h * LANES:(h + 1) * LANES] = d.astype(o_ref.dtype)


def _attention(qs, lam_vecs, sg, k, v, bias, lam_init, batch, seq):
    n_heads, tt, _ = k.shape
    t = ATT_TILE
    nq = seq // t
    kd = n_heads * LANES
    kv_spec = pl.BlockSpec((n_heads, seq, LANES), lambda b, i: (0, b, 0))
    return pl.pallas_call(
        functools.partial(_attn_kernel, lam_init), grid=(batch, nq),
        in_specs=[pl.BlockSpec((n_heads, None, 2 * t, LANES), lambda b, i: (0, b * nq + i, 0, 0)),
                  _const_spec(lam_vecs.shape), _const_spec((1, LANES)), kv_spec, kv_spec,
                  _const_spec(bias.shape)],
        out_specs=pl.BlockSpec((t, kd), lambda b, i: (b * nq + i, 0)),
        out_shape=jax.ShapeDtypeStruct((tt, kd), BF16),
        scratch_shapes=[pltpu.VMEM((n_heads, 2 * t, LANES), F32),
                        pltpu.VMEM((n_heads, 2 * t, 2 * LANES), F32)],
        compiler_params=_params("arbitrary", "arbitrary"), name="diff_attention",
    )(qs, lam_vecs, sg.reshape(1, LANES), k, v, bias)


def kernel(x, ssm_ln_g, ssm_in_w, ssm_conv_w, ssm_conv_b, ssm_dt_bias, ssm_a_log, ssm_d, ssm_norm_g, ssm_out_w, kv_ln_g, kv_w, k_norm_g, rel_bias, attn_ln_g, q_w, q_norm_g, lam_vecs, subln_g, attn_out_w, ffn_ln_g, ffn_up_w, ffn_conv_w, ffn_conv_b, ffn_down_w):
    batch, seq, d = x.shape
    t = batch * seq
    n_a = ssm_in_w.shape[0]
    depth = ffn_up_w.shape[0]
    h = x.reshape(t, d)
    k_sh = v_sh = bias = None
    n_att_heads = q_w.shape[2] // LANES
    ffn_up = ffn_up_w.astype(BF16)
    ffn_down = ffn_down_w.astype(BF16)

    for layer in range(depth):
        if layer < n_a:
            i = layer
            d_inner = ssm_out_w.shape[1]
            n_ssm_heads = ssm_dt_bias.shape[1]
            conv_dim = ssm_conv_w.shape[2]
            w_dt = jnp.pad(ssm_in_w[i][:, d_inner + conv_dim:].astype(BF16),
                           ((0, 0), (0, LANES - n_ssm_heads)))
            zg, xs, bm, cm, dt = _in_proj(h, ssm_ln_g[i], ssm_in_w[i].astype(BF16), w_dt, ssm_conv_w[i],
                                          ssm_conv_b[i].reshape(1, -1), d_inner, seq)
            pad_h = ((0, 0), (0, LANES - n_ssm_heads))
            mix = _ssd(zg, xs, bm, cm, dt, jnp.pad(ssm_dt_bias[i].reshape(1, -1), pad_h),
                       jnp.pad(ssm_a_log[i].reshape(1, -1), pad_h),
                       ssm_d[i], jnp.broadcast_to(ssm_norm_g[i][:, None], (d_inner, LANES)), batch, seq)
            w_mix = ssm_out_w[i].astype(BF16)
        else:
            j = layer - n_a
            if j == 0:
                bias = _bias_tiles(rel_bias, n_att_heads, ATT_TILE)
            k_new, v_new, qs = _qkv_proj(h, kv_ln_g, attn_ln_g[j], kv_w.astype(BF16), q_w[j].astype(BF16),
                                         jnp.tile(k_norm_g, 2).reshape(1, LANES),
                                         jnp.tile(q_norm_g[j], 2).reshape(1, LANES), n_att_heads, ATT_TILE)
            if j == 0:
                k_sh, v_sh = k_new, v_new
            lam_init = 0.8 - 0.6 * math.exp(-0.3 * layer)
            mix = _attention(qs, lam_vecs[j], subln_g[j], k_sh, v_sh, bias, lam_init, batch, seq)
            w_mix = attn_out_w[j].astype(BF16)
        h = _ffn(mix, w_mix, h, ffn_ln_g[layer], ffn_up, ffn_conv_w[layer], ffn_conv_b[layer],
                 ffn_down, layer, seq, f"conv_ffn_{layer}")
    return h.reshape(batch, seq, d)
```
